```python
import jax
import jax.numpy as jnp
from jax import lax
import numpy as np

D_MODEL = 1024
BATCH = 4
SEQ = 8192
DEPTH = 4

GRID_W = 64
CTX_LEN = 256
N_MIXERS = 3
MIX_CONV = 0
MIX_LRU = 1
MIX_NA = 2
N_CONV_LAYERS = (DEPTH + 2) // 3
N_LRU_LAYERS = (DEPTH + 1) // 3
N_NA_LAYERS = DEPTH // 3
CONV_K = 31
D_RNN = D_MODEL
RNN_BLOCKS = 8
RNN_BLOCK = D_RNN // RNN_BLOCKS
RNN_CONV_K = 4
LRU_C = 8.0
NA_HEADS = 16
NA_HEAD_DIM = D_MODEL // NA_HEADS
NA_KH = 8
NA_KW = 16
D_FF = -(-8 * D_MODEL // (3 * 256)) * 256
EPS = 1e-6

kernel_name = 'hybrid_conv_rglru_natten_dit'


def _rmsnorm(x, g):
    xf = x.astype(jnp.float32)
    y = xf * lax.rsqrt(jnp.mean(xf * xf, axis=-1, keepdims=True) + EPS)
    return (y * g.astype(jnp.float32)).astype(x.dtype)


def _layernorm(x, g, b):
    xf = x.astype(jnp.float32)
    mu = jnp.mean(xf, axis=-1, keepdims=True)
    var = jnp.mean(jnp.square(xf - mu), axis=-1, keepdims=True)
    y = (xf - mu) * lax.rsqrt(var + EPS) * g.astype(jnp.float32) + b.astype(jnp.float32)
    return y.astype(x.dtype)


def _modulation(cond, w_mod, b_mod):
    m = jax.nn.silu(cond) @ w_mod + b_mod
    return jnp.split(m, 6, axis=-1)


def _dwconv(x, w, pad_left, pad_right):
    return lax.conv_general_dilated(
        x, w[:, None, :], window_strides=(1,), padding=[(pad_left, pad_right)],
        dimension_numbers=('NWC', 'WIO', 'NWC'), feature_group_count=x.shape[-1])


def _swiglu(h, w_in, w_out):
    u = h @ w_in
    return (jax.nn.silu(u[..., :D_FF]) * u[..., D_FF:]) @ w_out


def _conformer_conv(h, w_pw1, b_pw1, w_dw, b_dw, ln_g, ln_b, w_pw2, b_pw2):
    u = h @ w_pw1 + b_pw1
    u = u[..., :D_MODEL] * jax.nn.sigmoid(u[..., D_MODEL:])
    u = _dwconv(u, w_dw, CONV_K // 2, CONV_K // 2) + b_dw
    u = jax.nn.silu(_layernorm(u, ln_g, ln_b))
    return u @ w_pw2 + b_pw2


def _lru_coeffs(xr, w_rg, b_rg, w_ig, b_ig, lam):
    shp = xr.shape
    xb = xr.reshape(shp[:-1] + (RNN_BLOCKS, RNN_BLOCK))
    r = jax.nn.sigmoid((jnp.einsum('btni,nij->btnj', xb, w_rg).reshape(shp) + b_rg).astype(jnp.float32))
    gi = jax.nn.sigmoid((jnp.einsum('btni,nij->btnj', xb, w_ig).reshape(shp) + b_ig).astype(jnp.float32))
    log_a = -LRU_C * r * jax.nn.softplus(-lam.astype(jnp.float32))
    a = jnp.exp(log_a)
    b = jnp.sqrt(-jnp.expm1(2.0 * log_a)) * (gi * xr.astype(jnp.float32))
    return a, b


def _linear_scan(a, b, h0, reverse, collect):
    def step(h, ab):
        a_t, b_t = ab
        h = a_t * h + b_t
        return h, (h if collect else None)
    h_last, hs = lax.scan(step, h0, (jnp.swapaxes(a, 0, 1), jnp.swapaxes(b, 0, 1)), reverse=reverse)
    hs = jnp.swapaxes(hs, 0, 1) if collect else None
    return hs, h_last


def _rglru_block(h_lat, h_ctx, ctx_out, w_in, b_in, w_conv, b_conv, w_rg, b_rg, w_ig, b_ig, lam, w_out, b_out):
    pad_l = RNN_CONV_K // 2
    pad_r = RNN_CONV_K - 1 - RNN_CONV_K // 2
    u_l = h_lat @ w_in + b_in
    u_c = h_ctx @ w_in + b_in
    xr_l = _dwconv(u_l[..., D_RNN:], w_conv, pad_l, pad_r) + b_conv
    xr_c = _dwconv(u_c[..., D_RNN:], w_conv, pad_l, pad_r) + b_conv
    h0 = jnp.zeros((h_lat.shape[0], D_RNN), jnp.float32)
    ys_l = []
    ys_c = []
    for d, reverse in ((0, False), (1, True)):
        a_c, b_c = _lru_coeffs(xr_c, w_rg[d], b_rg[d], w_ig[d], b_ig[d], lam[d])
        hs_c, h_c_last = _linear_scan(a_c, b_c, h0, reverse, ctx_out)
        a_l, b_l = _lru_coeffs(xr_l, w_rg[d], b_rg[d], w_ig[d], b_ig[d], lam[d])
        hs_l, _ = _linear_scan(a_l, b_l, h_c_last, reverse, True)
        ys_l.append(hs_l)
        ys_c.append(hs_c)
    y_l = (ys_l[0] + ys_l[1]).astype(h_lat.dtype) * jax.nn.gelu(u_l[..., :D_RNN])
    out_l = y_l @ w_out + b_out
    out_c = None
    if ctx_out:
        y_c = (ys_c[0] + ys_c[1]).astype(h_ctx.dtype) * jax.nn.gelu(u_c[..., :D_RNN])
        out_c = y_c @ w_out + b_out
    return out_l, out_c


def _na_attention(h_lat, h_ctx, ctx_out, w_qkv, b_qkv, rpb, w_o, b_o):
    B, S, _ = h_lat.shape
    rows = S // GRID_W
    kh = min(NA_KH, rows)
    scale = NA_HEAD_DIM ** -0.5

    def qkv(h):
        u = h @ w_qkv + b_qkv
        q, k, v = jnp.split(u, 3, axis=-1)
        shp = h.shape[:2] + (NA_HEADS, NA_HEAD_DIM)
        return q.reshape(shp), k.reshape(shp), v.reshape(shp)

    q, k, v = qkv(h_lat)
    qc, kc, vc = qkv(h_ctx)
    qg = q.reshape(B, rows, GRID_W, NA_HEADS, NA_HEAD_DIM)
    kg = k.reshape(B, rows, GRID_W, NA_HEADS, NA_HEAD_DIM)
    vg = v.reshape(B, rows, GRID_W, NA_HEADS, NA_HEAD_DIM)

    cols = jnp.arange(GRID_W)
    col_start = jnp.clip(cols - NA_KW // 2, 0, GRID_W - NA_KW)
    col_idx = col_start[:, None] + jnp.arange(NA_KW)[None, :]
    col_off = col_idx - cols[:, None] + (NA_KW - 1)
    bias_c = rpb[:, :, col_off].astype(jnp.float32)
    n_loc = kh * NA_KW

    def row_block(r):
        r0 = jnp.clip(r - kh // 2, 0, rows - kh)
        k_strip = lax.dynamic_slice_in_dim(kg, r0, kh, axis=1)
        v_strip = lax.dynamic_slice_in_dim(vg, r0, kh, axis=1)
        k_win = k_strip[:, :, col_idx]
        v_win = v_strip[:, :, col_idx]
        q_r = lax.dynamic_index_in_dim(qg, r, axis=1, keepdims=False)
        row_off = r0 + jnp.arange(kh) - r + (NA_KH - 1)
        bias = jnp.transpose(bias_c[:, row_off], (0, 2, 1, 3))
        s_loc = jnp.einsum('bqhd,brqchd->bhqrc', q_r, k_win).astype(jnp.float32) * scale + bias[None]
        s_ctx = jnp.einsum('bqhd,bkhd->bhqk', q_r, kc).astype(jnp.float32) * scale
        s = jnp.concatenate([s_loc.reshape(B, NA_HEADS, GRID_W, n_loc), s_ctx], axis=-1)
        p = jax.nn.softmax(s, axis=-1).astype(v.dtype)
        p_loc = p[..., :n_loc].reshape(B, NA_HEADS, GRID_W, kh, NA_KW)
        o = jnp.einsum('bhqrc,brqchd->bqhd', p_loc, v_win) + jnp.einsum('bhqk,bkhd->bqhd', p[..., n_loc:], vc)
        return o

    o = lax.map(row_block, jnp.arange(rows))
    o = jnp.transpose(o, (1, 0, 2, 3, 4)).reshape(B, S, D_MODEL)
    out_l = o @ w_o + b_o
    out_c = None
    if ctx_out:
        s_c = jnp.einsum('bqhd,bkhd->bhqk', qc, kc).astype(jnp.float32) * scale
        p_c = jax.nn.softmax(s_c, axis=-1).astype(vc.dtype)
        o_c = jnp.einsum('bhqk,bkhd->bqhd', p_c, vc).reshape(h_ctx.shape[0], h_ctx.shape[1], D_MODEL)
        out_c = o_c @ w_o + b_o
    return out_l, out_c


def setup_inputs(seed: int = 0) -> dict:
    key = jax.random.key(seed)
    ks = iter(jax.random.split(key, 48))

    def nrm(shape, scale):
        return jax.random.normal(next(ks), shape, jnp.float32) * scale

    d = D_MODEL
    x = nrm((BATCH, SEQ, d), 1.0)
    c = nrm((BATCH, d), 1.0)
    ctx = nrm((BATCH, CTX_LEN, d), 1.0)
    c_ctx = nrm((d,), 1.0)
    w_mod = nrm((DEPTH, d, 6 * d), 0.5 * d ** -0.5)
    b_mod = nrm((DEPTH, 6 * d), 0.02)
    norm_g = 1.0 + nrm((DEPTH, 2, d), 0.02)
    w_ffn_in = nrm((DEPTH, d, 2 * D_FF), d ** -0.5)
    w_ffn_out = nrm((DEPTH, D_FF, d), D_FF ** -0.5)
    a_w_pw1 = nrm((N_CONV_LAYERS, d, 2 * d), d ** -0.5)
    a_b_pw1 = nrm((N_CONV_LAYERS, 2 * d), 0.02)
    a_w_dw = nrm((N_CONV_LAYERS, CONV_K, d), CONV_K ** -0.5)
    a_b_dw = nrm((N_CONV_LAYERS, d), 0.02)
    a_ln_g = 1.0 + nrm((N_CONV_LAYERS, d), 0.02)
    a_ln_b = nrm((N_CONV_LAYERS, d), 0.02)
    a_w_pw2 = nrm((N_CONV_LAYERS, d, d), d ** -0.5)
    a_b_pw2 = nrm((N_CONV_LAYERS, d), 0.02)
    b_w_in = nrm((N_LRU_LAYERS, d, 2 * D_RNN), d ** -0.5)
    b_b_in = nrm((N_LRU_LAYERS, 2 * D_RNN), 0.02)
    b_w_conv = nrm((N_LRU_LAYERS, RNN_CONV_K, D_RNN), RNN_CONV_K ** -0.5)
    b_b_conv = nrm((N_LRU_LAYERS, D_RNN), 0.02)
    b_w_rg = nrm((N_LRU_LAYERS, 2, RNN_BLOCKS, RNN_BLOCK, RNN_BLOCK), RNN_BLOCK ** -0.5)
    b_b_rg = nrm((N_LRU_LAYERS, 2, D_RNN), 0.02)
    b_w_ig = nrm((N_LRU_LAYERS, 2, RNN_BLOCKS, RNN_BLOCK, RNN_BLOCK), RNN_BLOCK ** -0.5)
    b_b_ig = nrm((N_LRU_LAYERS, 2, D_RNN), 0.02)
    a0 = jax.random.uniform(next(ks), (N_LRU_LAYERS, 2, D_RNN), jnp.float32, minval=0.9, maxval=0.999)
    s0 = a0 ** (1.0 / LRU_C)
    b_lam = jnp.log(s0) - jnp.log1p(-s0)
    b_w_out = nrm((N_LRU_LAYERS, D_RNN, d), D_RNN ** -0.5)
    b_b_out = nrm((N_LRU_LAYERS, d), 0.02)
    c_w_qkv = nrm((N_NA_LAYERS, d, 3 * d), d ** -0.5)
    c_b_qkv = nrm((N_NA_LAYERS, 3 * d), 0.02)
    c_rpb = nrm((N_NA_LAYERS, NA_HEADS, 2 * NA_KH - 1, 2 * NA_KW - 1), 0.5)
    c_w_o = nrm((N_NA_LAYERS, d, d), d ** -0.5)
    c_b_o = nrm((N_NA_LAYERS, d), 0.02)
    final_g = 1.0 + nrm((d,), 0.02)
    return {'x': x, 'c': c, 'ctx': ctx, 'c_ctx': c_ctx,
            'w_mod': w_mod, 'b_mod': b_mod, 'norm_g': norm_g,
            'w_ffn_in': w_ffn_in, 'w_ffn_out': w_ffn_out,
            'a_w_pw1': a_w_pw1, 'a_b_pw1': a_b_pw1, 'a_w_dw': a_w_dw, 'a_b_dw': a_b_dw,
            'a_ln_g': a_ln_g, 'a_ln_b': a_ln_b, 'a_w_pw2': a_w_pw2, 'a_b_pw2': a_b_pw2,
            'b_w_in': b_w_in, 'b_b_in': b_b_in, 'b_w_conv': b_w_conv, 'b_b_conv': b_b_conv,
            'b_w_rg': b_w_rg, 'b_b_rg': b_b_rg, 'b_w_ig': b_w_ig, 'b_b_ig': b_b_ig,
            'b_lam': b_lam, 'b_w_out': b_w_out, 'b_b_out': b_b_out,
            'c_w_qkv': c_w_qkv, 'c_b_qkv': c_b_qkv, 'c_rpb': c_rpb, 'c_w_o': c_w_o, 'c_b_o': c_b_o,
            'final_g': final_g}


def reference(x, c, ctx, c_ctx, w_mod, b_mod, norm_g, w_ffn_in, w_ffn_out,
              a_w_pw1, a_b_pw1, a_w_dw, a_b_dw, a_ln_g, a_ln_b, a_w_pw2, a_b_pw2,
              b_w_in, b_b_in, b_w_conv, b_b_conv, b_w_rg, b_b_rg, b_w_ig, b_b_ig,
              b_lam, b_w_out, b_b_out,
              c_w_qkv, c_b_qkv, c_rpb, c_w_o, c_b_o, final_g):
    cs = ctx
    for i in range(DEPTH):
        kind = i % N_MIXERS
        j = i // N_MIXERS
        ctx_out = any(k % N_MIXERS != MIX_CONV for k in range(i + 1, DEPTH))
        ctx_in = ctx_out or kind != MIX_CONV
        sh1, sc1, g1, sh2, sc2, g2 = _modulation(c[:, None, :], w_mod[i], b_mod[i])
        h = _rmsnorm(x, norm_g[i, 0]) * (1 + sc1) + sh1
        hc = None
        if ctx_in:
            csh1, csc1, cg1, csh2, csc2, cg2 = _modulation(c_ctx, w_mod[i], b_mod[i])
            hc = _rmsnorm(cs, norm_g[i, 0]) * (1 + csc1) + csh1
        if kind == MIX_CONV:
            conv_p = (a_w_pw1[j], a_b_pw1[j], a_w_dw[j], a_b_dw[j], a_ln_g[j], a_ln_b[j], a_w_pw2[j], a_b_pw2[j])
            y = _conformer_conv(h, *conv_p)
            yc = _conformer_conv(hc, *conv_p) if ctx_out else None
        elif kind == MIX_LRU:
            y, yc = _rglru_block(h, hc, ctx_out, b_w_in[j], b_b_in[j], b_w_conv[j], b_b_conv[j],
                                 b_w_rg[j], b_b_rg[j], b_w_ig[j], b_b_ig[j], b_lam[j], b_w_out[j], b_b_out[j])
        else:
            y, yc = _na_attention(h, hc, ctx_out, c_w_qkv[j], c_b_qkv[j], c_rpb[j], c_w_o[j], c_b_o[j])
        x = x + g1 * y
        x = x + g2 * _swiglu(_rmsnorm(x, norm_g[i, 1]) * (1 + sc2) + sh2, w_ffn_in[i], w_ffn_out[i])
        if ctx_out:
            cs = cs + cg1 * yc
            cs = cs + cg2 * _swiglu(_rmsnorm(cs, norm_g[i, 1]) * (1 + csc2) + csh2, w_ffn_in[i], w_ffn_out[i])
    return _rmsnorm(x, final_g)
```

```python
import functools

import jax
import jax.numpy as jnp
from jax import lax
from jax.experimental import pallas as pl
from jax.experimental.pallas import tpu as pltpu

D = 1024
DEPTH = 4
GRID_W = 64
CTX = 256
CONV_K = 31
RNN_BLOCKS = 8
RNN_BLOCK = 128
RNN_CONV_K = 4
LRU_C = 8.0
HEADS = 16
HD = 64
NA_KH = 8
NA_KW = 16
DFF = 2816
EPS = 1e-6

LANES = 128
SUBLANES = 8
VMEM_LIMIT = 56 * 1024 * 1024
NEG = -1e30

BF = jnp.bfloat16
F32 = jnp.float32


def _dot(a, b):
    return jnp.dot(a, b, preferred_element_type=F32)


def _params(n_axes):
    return pltpu.CompilerParams(
        dimension_semantics=("arbitrary",) * n_axes, vmem_limit_bytes=VMEM_LIMIT)


def _const_spec(shape):
    nd = len(shape)
    return pl.BlockSpec(shape, lambda *_: (0,) * nd, pipeline_mode=pl.Buffered(1))


def _row_tile(t):
    return min(512, t)


def _normmod(x, g, sc, sh):
    ms = jnp.mean(x * x, axis=-1, keepdims=True)
    return (x * lax.rsqrt(ms + EPS) * g) * (1.0 + sc) + sh


def _sigmoid(x):
    return jax.nn.sigmoid(x)


def _gelu_tanh(x):
    return 0.5 * x * (1.0 + jnp.tanh(0.7978845608028654 * (x + 0.044715 * (x * x * x))))


MOD_TN = 1536


def _mod_kernel(c_ref, w_ref, b_ref, o_ref):
    cnd = c_ref[...]
    s = (cnd * _sigmoid(cnd)).astype(BF)
    o_ref[0] = _dot(s, w_ref[0].astype(BF)) + b_ref[0]


def _mod_call(cond, w_mod, b_mod):
    n = 6 * D
    return pl.pallas_call(
        _mod_kernel,
        grid=(DEPTH, n // MOD_TN),
        in_specs=[
            pl.BlockSpec((SUBLANES, D), lambda i, j: (0, 0)),
            pl.BlockSpec((1, D, MOD_TN), lambda i, j: (i, 0, j)),
            pl.BlockSpec((1, 1, MOD_TN), lambda i, j: (i, 0, j)),
        ],
        out_specs=pl.BlockSpec((1, SUBLANES, MOD_TN), lambda i, j: (i, 0, j)),
        out_shape=jax.ShapeDtypeStruct((DEPTH, SUBLANES, n), F32),
        compiler_params=_params(2),
    )(cond, w_mod, b_mod.reshape(DEPTH, 1, n))


PROJ_CH = 512


def _in_proj_kernel(x_ref, mod_ref, g_ref, w_ref, b_ref, *o_refs, variant):
    mod = mod_ref[0]
    h = _normmod(x_ref[0], g_ref[...], mod[1:2], mod[0:1]).astype(BF)

    def proj(c0):
        return _dot(h, w_ref[:, c0:c0 + PROJ_CH]) + b_ref[:, c0:c0 + PROJ_CH]

    for j in range(D // PROJ_CH):
        c0 = j * PROJ_CH
        if variant == "glu":
            o_refs[0][0, :, c0:c0 + PROJ_CH] = proj(c0) * _sigmoid(proj(D + c0))
        elif variant == "lru":
            o_refs[0][0, :, c0:c0 + PROJ_CH] = _gelu_tanh(proj(c0))
            o_refs[1][0, :, c0:c0 + PROJ_CH] = proj(D + c0)
        else:
            o_refs[0][0, :, c0:c0 + PROJ_CH] = (proj(c0) * (HD ** -0.5)).astype(BF)
            o_refs[0][0, :, D + c0:D + c0 + PROJ_CH] = proj(D + c0).astype(BF)
            o_refs[0][0, :, 2 * D + c0:2 * D + c0 + PROJ_CH] = proj(2 * D + c0).astype(BF)


def _in_proj(x, mod, g, w, b, variant):
    bsz, t, _ = x.shape
    tm = _row_tile(t)
    n = w.shape[1]
    row = lambda width: pl.BlockSpec((1, tm, width), lambda bi, ti: (bi, ti, 0))
    if variant == "glu":
        out_shape = [jax.ShapeDtypeStruct((bsz, t, D), F32)]
        out_specs = [row(D)]
    elif variant == "lru":
        out_shape = [jax.ShapeDtypeStruct((bsz, t, D), F32)] * 2
        out_specs = [row(D), row(D)]
    else:
        out_shape = [jax.ShapeDtypeStruct((bsz, t, 3 * D), BF)]
        out_specs = [row(3 * D)]
    outs = pl.pallas_call(
        functools.partial(_in_proj_kernel, variant=variant),
        grid=(bsz, t // tm),
        in_specs=[
            row(D),
            pl.BlockSpec((1, SUBLANES, D), lambda bi, ti: (bi, 0, 0)),
            _const_spec((1, D)),
            _const_spec((D, n)),
            _const_spec((1, n)),
        ],
        out_specs=out_specs,
        out_shape=out_shape,
        compiler_params=_params(2),
    )(x, mod, g.reshape(1, D), w, b.reshape(1, n))
    return outs


CONV_HALO = 16
CONV_RC = 64
CONV_TM = 256
CONV_SPAN = (CONV_K - 1) // SUBLANES * SUBLANES


def _conv_kernel(x_ref, up_ref, uc_ref, un_ref, mod_ref, wdw_ref, bdw_ref, lng_ref, lnb_ref,
                 w2_ref, b2_ref, o_ref, buf_ref, sh_ref, acc_ref, *, tm, nt):
    t = pl.program_id(1)
    buf_ref[0:CONV_HALO, :] = jnp.where(t > 0, up_ref[0], 0.0)
    buf_ref[CONV_HALO:CONV_HALO + tm, :] = uc_ref[0]
    buf_ref[CONV_HALO + tm:2 * CONV_HALO + tm, :] = jnp.where(t < nt - 1, un_ref[0], 0.0)
    span = tm + CONV_SPAN
    for s in range(SUBLANES):
        for cb in range(D // LANES):
            lanes = slice(cb * LANES, (cb + 1) * LANES)
            sh_ref[s, :, lanes] = buf_ref[s:s + span, lanes]
    off = CONV_HALO - CONV_K // 2

    def rows(rc, carry):
        base = pl.multiple_of(rc * CONV_RC, CONV_RC)
        for cb in range(D // LANES):
            lanes = slice(cb * LANES, (cb + 1) * LANES)
            acc = jnp.zeros((CONV_RC, LANES), F32)
            for k in range(CONV_K):
                s = (off + k) % SUBLANES
                q = off + k - s
                acc = acc + sh_ref[s, pl.ds(base + q, CONV_RC), lanes] * wdw_ref[k:k + 1, lanes]
            acc_ref[pl.ds(base, CONV_RC), lanes] = acc + bdw_ref[:, lanes]
        return carry

    lax.fori_loop(0, tm // CONV_RC, rows, 0)

    v = acc_ref[...]
    mu = jnp.mean(v, axis=-1, keepdims=True)
    vc = v - mu
    var = jnp.mean(vc * vc, axis=-1, keepdims=True)
    y = vc * lax.rsqrt(var + EPS) * lng_ref[...] + lnb_ref[...]
    s = (y * _sigmoid(y)).astype(BF)
    y2 = _dot(s, w2_ref[...]) + b2_ref[...]
    o_ref[0] = x_ref[0] + mod_ref[0][2:3] * y2


def _conv_block(x, u, mod, w_dw, b_dw, ln_g, ln_b, w2, b2):
    bsz, t, _ = x.shape
    tm = CONV_TM
    nt = t // tm
    hb = tm // CONV_HALO
    row = pl.BlockSpec((1, tm, D), lambda bi, ti: (bi, ti, 0))
    return pl.pallas_call(
        functools.partial(_conv_kernel, tm=tm, nt=nt),
        grid=(bsz, nt),
        in_specs=[
            row,
            pl.BlockSpec((1, CONV_HALO, D), lambda bi, ti: (bi, jnp.maximum(ti * hb - 1, 0), 0)),
            row,
            pl.BlockSpec((1, CONV_HALO, D),
                         lambda bi, ti: (bi, jnp.minimum((ti + 1) * hb, t // CONV_HALO - 1), 0)),
            pl.BlockSpec((1, SUBLANES, D), lambda bi, ti: (bi, 0, 0)),
            _const_spec((CONV_K, D)),
            _const_spec((1, D)),
            _const_spec((1, D)),
            _const_spec((1, D)),
            _const_spec((D, D)),
            _const_spec((1, D)),
        ],
        out_specs=row,
        out_shape=jax.ShapeDtypeStruct((bsz, t, D), F32),
        scratch_shapes=[pltpu.VMEM((tm + 2 * CONV_HALO, D), F32),
                        pltpu.VMEM((SUBLANES, tm + CONV_SPAN, D), F32),
                        pltpu.VMEM((tm, D), F32)],
        compiler_params=_params(2),
    )(x, u, u, u, mod, w_dw, b_dw.reshape(1, D), ln_g.reshape(1, D), ln_b.reshape(1, D),
      w2, b2.reshape(1, D))


FFN_CH = 256


def _ffn_kernel(x_ref, mod_ref, g_ref, win_ref, wout_ref, fg_ref, o_ref, acc_ref, *, final):
    x = x_ref[0]
    mod = mod_ref[0]
    h = _normmod(x, g_ref[...], mod[4:5], mod[3:4]).astype(BF)
    for j in range(DFF // FFN_CH):
        c0 = j * FFN_CH
        u1 = _dot(h, win_ref[:, c0:c0 + FFN_CH])
        u2 = _dot(h, win_ref[:, DFF + c0:DFF + c0 + FFN_CH])
        a = ((u1 * _sigmoid(u1)) * u2).astype(BF)
        part = _dot(a, wout_ref[c0:c0 + FFN_CH, :])
        if j == 0:
            acc_ref[...] = part
        else:
            acc_ref[...] += part
    out = x + mod[5:6] * acc_ref[...]
    if final:
        ms = jnp.mean(out * out, axis=-1, keepdims=True)
        out = out * lax.rsqrt(ms + EPS) * fg_ref[...]
    o_ref[0] = out


def _ffn(x, mod, g, w_in, w_out, final_g=None):
    bsz, t, _ = x.shape
    tm = _row_tile(t)
    final = final_g is not None
    fg = (final_g if final else jnp.ones((D,), F32)).reshape(1, D)
    row = pl.BlockSpec((1, tm, D), lambda bi, ti: (bi, ti, 0))
    return pl.pallas_call(
        functools.partial(_ffn_kernel, final=final),
        grid=(bsz, t // tm),
        in_specs=[
            row,
            pl.BlockSpec((1, SUBLANES, D), lambda bi, ti: (bi, 0, 0)),
            _const_spec((1, D)),
            _const_spec((D, 2 * DFF)),
            _const_spec((DFF, D)),
            _const_spec((1, D)),
        ],
        out_specs=row,
        out_shape=jax.ShapeDtypeStruct((bsz, t, D), F32),
        scratch_shapes=[pltpu.VMEM((tm, D), F32)],
        compiler_params=_params(2),
    )(x, mod, g.reshape(1, D), w_in, w_out, fg)


LRU_HALO = 8


def _lru_scan_kernel(xp_ref, xc_ref, xn_ref, h0_ref, wc_ref, bc_ref, wg_ref, bg_ref, lam_ref,
                     hs_ref, hl_ref, xbuf_ref, pc_ref, bs_ref, h_ref, *, tm, nt, reverse):
    t = pl.program_id(1)
    tt = (nt - 1 - t) if reverse else t

    @pl.when(t == 0)
    def _():
        h_ref[...] = jnp.broadcast_to(h0_ref[0], (SUBLANES, D))

    xbuf_ref[0:LRU_HALO, :] = jnp.where(tt > 0, xp_ref[0], 0.0)
    xbuf_ref[LRU_HALO:LRU_HALO + tm, :] = xc_ref[0]
    xbuf_ref[LRU_HALO + tm:2 * LRU_HALO + tm, :] = jnp.where(tt < nt - 1, xn_ref[0], 0.0)
    off = LRU_HALO - RNN_CONV_K // 2
    nv = tm // SUBLANES
    sub = lax.broadcasted_iota(jnp.int32, (nv, SUBLANES, LANES), 1)

    for n in range(RNN_BLOCKS):
        lanes = slice(n * LANES, (n + 1) * LANES)
        xr = jnp.zeros((tm, LANES), F32)
        for k in range(RNN_CONV_K):
            xr = xr + xbuf_ref[off + k:off + k + tm, lanes] * wc_ref[k:k + 1, lanes]
        xr = xr + bc_ref[:, lanes]
        gts = _dot(xr.astype(BF), wg_ref[n]) + bg_ref[n]
        r = _sigmoid(gts[:, :LANES])
        gi = _sigmoid(gts[:, LANES:])
        nl = -lam_ref[:, lanes]
        sp = jnp.maximum(nl, 0.0) + jnp.log1p(jnp.exp(-jnp.abs(nl)))
        log_a = (-LRU_C) * r * sp
        a = jnp.exp(log_a)
        th = jnp.tanh(log_a)
        one_m_a2 = (-2.0 * th) / (1.0 - th)
        b = jnp.sqrt(one_m_a2) * (gi * xr)
        a3 = a.reshape(nv, SUBLANES, LANES)
        b3 = b.reshape(nv, SUBLANES, LANES)
        for sh in (1, 2, 4):
            if reverse:
                m = sub < SUBLANES - sh
                a_s = pltpu.roll(a3, SUBLANES - sh, 1)
                b_s = pltpu.roll(b3, SUBLANES - sh, 1)
            else:
                m = sub >= sh
                a_s = pltpu.roll(a3, sh, 1)
                b_s = pltpu.roll(b3, sh, 1)
            b3 = jnp.where(m, a3 * b_s + b3, b3)
            a3 = jnp.where(m, a3 * a_s, a3)
        pc_ref[:, lanes] = a3.reshape(tm, LANES)
        bs_ref[:, lanes] = b3.reshape(tm, LANES)

    edge = 0 if reverse else SUBLANES - 1

    def group(i, hcar):
        v = (nv - 1 - i) if reverse else i
        st = pl.multiple_of(v * SUBLANES, SUBLANES)
        res = bs_ref[pl.ds(st, SUBLANES), :] + pc_ref[pl.ds(st, SUBLANES), :] * hcar
        hs_ref[0, pl.ds(st, SUBLANES), :] = res
        return jnp.broadcast_to(res[edge:edge + 1, :], (SUBLANES, D))

    hfin = lax.fori_loop(0, nv, group, h_ref[...])
    h_ref[...] = hfin
    hl_ref[0] = hfin[0:1, :]


def _lru_scan(xpre, h0, w_conv, b_conv, wg, bg, lam, reverse):
    bsz, t, _ = xpre.shape
    tm = _row_tile(t)
    nt = t // tm
    hb = tm // LRU_HALO
    tidx = (lambda ti: nt - 1 - ti) if reverse else (lambda ti: ti)
    row = pl.BlockSpec((1, tm, D), lambda bi, ti: (bi, tidx(ti), 0))
    hs, hl = pl.pallas_call(
        functools.partial(_lru_scan_kernel, tm=tm, nt=nt, reverse=reverse),
        grid=(bsz, nt),
        in_specs=[
            pl.BlockSpec((1, LRU_HALO, D), lambda bi, ti: (bi, jnp.maximum(tidx(ti) * hb - 1, 0), 0)),
            row,
            pl.BlockSpec((1, LRU_HALO, D),
                         lambda bi, ti: (bi, jnp.minimum((tidx(ti) + 1) * hb, t // LRU_HALO - 1), 0)),
            pl.BlockSpec((1, 1, D), lambda bi, ti: (bi, 0, 0)),
            _const_spec((RNN_CONV_K, D)),
            _const_spec((1, D)),
            _const_spec((RNN_BLOCKS, RNN_BLOCK, 2 * RNN_BLOCK)),
            _const_spec((RNN_BLOCKS, 1, 2 * RNN_BLOCK)),
            _const_spec((1, D)),
        ],
        out_specs=[row, pl.BlockSpec((1, 1, D), lambda bi, ti: (bi, 0, 0))],
        out_shape=[jax.ShapeDtypeStruct((bsz, t, D), F32), jax.ShapeDtypeStruct((bsz, 1, D), F32)],
        scratch_shapes=[pltpu.VMEM((tm + 2 * LRU_HALO, D), F32), pltpu.VMEM((tm, D), F32),
                        pltpu.VMEM((tm, D), F32), pltpu.VMEM((SUBLANES, D), F32)],
        compiler_params=_params(2),
    )(xpre, xpre, xpre, h0, w_conv, b_conv.reshape(1, D), wg, bg, lam.reshape(1, D))
    return hs, hl


def _out_proj_kernel(x_ref, mod_ref, *refs, lru):
    if lru:
        hf_ref, hr_ref, gate_ref, w_ref, b_ref, o_ref = refs
        y = ((hf_ref[0] + hr_ref[0]) * gate_ref[0]).astype(BF)
    else:
        y_ref, w_ref, b_ref, o_ref = refs
        y = y_ref[0]
    o_ref[0] = x_ref[0] + mod_ref[0][2:3] * (_dot(y, w_ref[...]) + b_ref[...])


def _out_proj(x, mod, ys, w, b):
    bsz, t, _ = x.shape
    tm = _row_tile(t)
    row = pl.BlockSpec((1, tm, D), lambda bi, ti: (bi, ti, 0))
    return pl.pallas_call(
        functools.partial(_out_proj_kernel, lru=len(ys) == 3),
        grid=(bsz, t // tm),
        in_specs=[row, pl.BlockSpec((1, SUBLANES, D), lambda bi, ti: (bi, 0, 0))]
        + [row] * len(ys) + [_const_spec((D, D)), _const_spec((1, D))],
        out_specs=row,
        out_shape=jax.ShapeDtypeStruct((bsz, t, D), F32),
        compiler_params=_params(2),
    )(x, mod, *ys, w, b.reshape(1, D))


NA_QR = 4
NA_SR = 12
NA_HG = 4
NA_NQ = NA_QR * GRID_W
NA_NK = NA_SR * GRID_W
NA_ROWS = 8192 // GRID_W


def _na_kernel(q_ref, k_ref, v_ref, kc_ref, vc_ref, bias_ref, o_ref, *, nr):
    ri = pl.program_id(2)
    rs = jnp.clip(ri * NA_QR - NA_KH // 2, 0, NA_ROWS - NA_SR)
    st = pl.multiple_of(rs * GRID_W, GRID_W)
    outs = []
    for h in range(NA_HG):
        lanes = slice(h * HD, (h + 1) * HD)
        q = q_ref[0, :, lanes]
        ks = k_ref[0, pl.ds(st, NA_NK), lanes]
        vs = v_ref[0, pl.ds(st, NA_NK), lanes]
        s_loc = lax.dot_general(q, ks, (((1,), (1,)), ((), ())), preferred_element_type=F32)
        s_loc = s_loc + bias_ref[0, h]
        s_ctx = lax.dot_general(q, kc_ref[0, :, lanes], (((1,), (1,)), ((), ())),
                                preferred_element_type=F32)
        m = jnp.maximum(jnp.max(s_loc, axis=-1, keepdims=True), jnp.max(s_ctx, axis=-1, keepdims=True))
        p_loc = jnp.exp(s_loc - m)
        p_ctx = jnp.exp(s_ctx - m)
        l = jnp.sum(p_loc, axis=-1, keepdims=True) + jnp.sum(p_ctx, axis=-1, keepdims=True)
        o = _dot(p_loc.astype(BF), vs) + _dot(p_ctx.astype(BF), vc_ref[0, :, lanes])
        outs.append(o / l)
    o_ref[0] = jnp.concatenate(outs, axis=-1).astype(BF)


def _na_bias_tables(rpb):
    rows = NA_ROWS
    cols = jnp.arange(GRID_W)
    c0 = jnp.clip(cols - NA_KW // 2, 0, GRID_W - NA_KW)
    kc = jnp.arange(GRID_W)
    col_ok = (kc[None, :] >= c0[:, None]) & (kc[None, :] < c0[:, None] + NA_KW)
    col_off = jnp.clip(kc[None, :] - cols[:, None] + (NA_KW - 1), 0, 2 * NA_KW - 2)
    tabs = []
    for r in (0, NA_QR, rows - NA_QR):
        rs = min(max(r - NA_KH // 2, 0), rows - NA_SR)
        qr = r + jnp.arange(NA_QR)
        r0 = jnp.clip(qr - NA_KH // 2, 0, rows - NA_KH)
        kr = rs + jnp.arange(NA_SR)
        row_ok = (kr[None, :] >= r0[:, None]) & (kr[None, :] < r0[:, None] + NA_KH)
        row_off = jnp.clip(kr[None, :] - qr[:, None] + (NA_KH - 1), 0, 2 * NA_KH - 2)
        g = rpb[:, row_off][:, :, :, col_off]
        ok = row_ok[:, :, None, None] & col_ok[None, None, :, :]
        g = jnp.where(ok[None], g, NEG)
        g = jnp.transpose(g, (0, 1, 3, 2, 4)).reshape(HEADS, NA_NQ, NA_NK)
        tabs.append(g)
    return jnp.stack(tabs)


def _na_attention(qkv, qkv_c, bias_tabs):
    bsz, s, _ = qkv.shape
    nr = NA_ROWS // NA_QR
    hw = NA_HG * HD
    nhg = HEADS // NA_HG

    def case(ri):
        return jnp.where(ri == 0, 0, jnp.where(ri == nr - 1, 2, 1))

    return pl.pallas_call(
        functools.partial(_na_kernel, nr=nr),
        grid=(bsz, nhg, nr),
        in_specs=[
            pl.BlockSpec((1, NA_NQ, hw), lambda bi, hg, ri: (bi, ri, hg)),
            pl.BlockSpec((1, s, hw), lambda bi, hg, ri: (bi, 0, nhg + hg)),
            pl.BlockSpec((1, s, hw), lambda bi, hg, ri: (bi, 0, 2 * nhg + hg)),
            pl.BlockSpec((1, CTX, hw), lambda bi, hg, ri: (bi, 0, nhg + hg)),
            pl.BlockSpec((1, CTX, hw), lambda bi, hg, ri: (bi, 0, 2 * nhg + hg)),
            pl.BlockSpec((1, NA_HG, NA_NQ, NA_NK), lambda bi, hg, ri: (case(ri), hg, 0, 0)),
        ],
        out_specs=pl.BlockSpec((1, NA_NQ, hw), lambda bi, hg, ri: (bi, ri, hg)),
        out_shape=jax.ShapeDtypeStruct((bsz, s, D), BF),
        compiler_params=_params(3),
    )(qkv, qkv, qkv, qkv_c, qkv_c, bias_tabs)


def kernel(x, c, ctx, c_ctx, w_mod, b_mod, norm_g, w_ffn_in, w_ffn_out, a_w_pw1, a_b_pw1, a_w_dw, a_b_dw, a_ln_g, a_ln_b, a_w_pw2, a_b_pw2, b_w_in, b_b_in, b_w_conv, b_b_conv, b_w_rg, b_b_rg, b_w_ig, b_b_ig, b_lam, b_w_out, b_b_out, c_w_qkv, c_b_qkv, c_rpb, c_w_o, c_b_o, final_g):
    bsz = x.shape[0]
    cond = jnp.concatenate([c, c_ctx[None], jnp.zeros((SUBLANES - bsz - 1, D), F32)], axis=0)
    mods = _mod_call(cond, w_mod, b_mod)

    def layer_mods(i):
        m = mods[i].reshape(SUBLANES, 6, D)
        pad = ((0, 0), (0, SUBLANES - 6), (0, 0))
        lat = jnp.pad(m[:bsz], pad)
        cx = jnp.pad(jnp.broadcast_to(m[bsz:bsz + 1], (bsz, 6, D)), pad)
        return lat, cx

    def conv_layer(xs, mod, i, j):
        (u,) = _in_proj(xs, mod, norm_g[i, 0], a_w_pw1[j].astype(BF), a_b_pw1[j], "glu")
        return _conv_block(xs, u, mod, a_w_dw[j], a_b_dw[j], a_ln_g[j], a_ln_b[j],
                           a_w_pw2[j].astype(BF), a_b_pw2[j])

    def ffn(xs, mod, i, final_g=None):
        return _ffn(xs, mod, norm_g[i, 1], w_ffn_in[i].astype(BF), w_ffn_out[i].astype(BF), final_g)

    cs = ctx

    lat, cx = layer_mods(0)
    x = ffn(conv_layer(x, lat, 0, 0), lat, 0)
    cs = ffn(conv_layer(cs, cx, 0, 0), cx, 0)

    lat, cx = layer_mods(1)
    w_in = b_w_in[0].astype(BF)
    w_out = b_w_out[0].astype(BF)
    gate_l, xpre_l = _in_proj(x, lat, norm_g[1, 0], w_in, b_b_in[0], "lru")
    gate_c, xpre_c = _in_proj(cs, cx, norm_g[1, 0], w_in, b_b_in[0], "lru")
    hs_l, hs_c = [], []
    for d, reverse in ((0, False), (1, True)):
        wg = jnp.concatenate([b_w_rg[0, d], b_w_ig[0, d]], axis=-1).astype(BF)
        bg = jnp.concatenate([b_b_rg[0, d].reshape(RNN_BLOCKS, 1, RNN_BLOCK),
                              b_b_ig[0, d].reshape(RNN_BLOCKS, 1, RNN_BLOCK)], axis=-1)
        h0 = jnp.zeros((bsz, 1, D), F32)
        hc, hc_last = _lru_scan(xpre_c, h0, b_w_conv[0], b_b_conv[0], wg, bg, b_lam[0, d], reverse)
        hl, _ = _lru_scan(xpre_l, hc_last, b_w_conv[0], b_b_conv[0], wg, bg, b_lam[0, d], reverse)
        hs_c.append(hc)
        hs_l.append(hl)
    x = ffn(_out_proj(x, lat, (hs_l[0], hs_l[1], gate_l), w_out, b_b_out[0]), lat, 1)
    cs = ffn(_out_proj(cs, cx, (hs_c[0], hs_c[1], gate_c), w_out, b_b_out[0]), cx, 1)

    lat, cx = layer_mods(2)
    w_qkv = c_w_qkv[0].astype(BF)
    (qkv,) = _in_proj(x, lat, norm_g[2, 0], w_qkv, c_b_qkv[0], "qkv")
    (qkv_c,) = _in_proj(cs, cx, norm_g[2, 0], w_qkv, c_b_qkv[0], "qkv")
    o = _na_attention(qkv, qkv_c, _na_bias_tables(c_rpb[0]))
    x = ffn(_out_proj(x, lat, (o,), c_w_o[0].astype(BF), c_b_o[0]), lat, 2)

    lat, _ = layer_mods(3)
    x = ffn(conv_layer(x, lat, 3, 1), lat, 3, final_g)
    return x
```

```python
import functools

import jax
import jax.numpy as jnp
import numpy as np
from jax import lax
from jax.experimental import pallas as pl
from jax.experimental.pallas import tpu as pltpu

D = 1024
DEPTH = 4
GRID_W = 64
CTX = 256
CONV_K = 31
RNN_BLOCKS = 8
RNN_BLOCK = 128
RNN_CONV_K = 4
LRU_C = 8.0
HEADS = 16
HD = 64
NA_KH = 8
NA_KW = 16
DFF = 2816
EPS = 1e-6

LANES = 128
SUBLANES = 8
VMEM_LIMIT = 56 * 1024 * 1024
NEG = -1e30

BF = jnp.bfloat16
F32 = jnp.float32


def _dot(a, b):
    return jnp.dot(a, b, preferred_element_type=F32)


def _params(n_axes):
    return pltpu.CompilerParams(
        dimension_semantics=("arbitrary",) * n_axes, vmem_limit_bytes=VMEM_LIMIT)


def _const_spec(shape):
    nd = len(shape)
    return pl.BlockSpec(shape, lambda *_: (0,) * nd, pipeline_mode=pl.Buffered(1))


def _row_tile(t):
    return min(512, t)


def _normmod(x, g, sc, sh):
    ms = jnp.mean(x * x, axis=-1, keepdims=True)
    return (x * lax.rsqrt(ms + EPS) * g) * (1.0 + sc) + sh


def _sigmoid(x):
    return 0.5 * jnp.tanh(0.5 * x) + 0.5


def _gelu_tanh(x):
    return 0.5 * x * (1.0 + jnp.tanh(0.7978845608028654 * (x + 0.044715 * (x * x * x))))


MOD_TN = 1536


def _mod_kernel(c_ref, w_ref, b_ref, o_ref):
    cnd = c_ref[...]
    s = (cnd * _sigmoid(cnd)).astype(BF)
    o_ref[0] = _dot(s, w_ref[0].astype(BF)) + b_ref[0]


def _mod_call(cond, w_mod, b_mod):
    n = 6 * D
    return pl.pallas_call(
        _mod_kernel,
        grid=(DEPTH, n // MOD_TN),
        in_specs=[
            pl.BlockSpec((SUBLANES, D), lambda i, j: (0, 0)),
            pl.BlockSpec((1, D, MOD_TN), lambda i, j: (i, 0, j)),
            pl.BlockSpec((1, 1, MOD_TN), lambda i, j: (i, 0, j)),
        ],
        out_specs=pl.BlockSpec((1, SUBLANES, MOD_TN), lambda i, j: (i, 0, j)),
        out_shape=jax.ShapeDtypeStruct((DEPTH, SUBLANES, n), F32),
        compiler_params=_params(2),
    )(cond, w_mod, b_mod.reshape(DEPTH, 1, n))


PROJ_CH = 512


def _in_proj_kernel(x_ref, mod_ref, g_ref, w_ref, b_ref, *o_refs, variant):
    mod = mod_ref[0]
    h = _normmod(x_ref[0], g_ref[...], mod[1:2], mod[0:1]).astype(BF)

    def proj(c0):
        return _dot(h, w_ref[:, c0:c0 + PROJ_CH]) + b_ref[:, c0:c0 + PROJ_CH]

    for j in range(D // PROJ_CH):
        c0 = j * PROJ_CH
        if variant == "glu":
            o_refs[0][0, :, c0:c0 + PROJ_CH] = proj(c0) * _sigmoid(proj(D + c0))
        elif variant == "lru":
            o_refs[0][0, :, c0:c0 + PROJ_CH] = _gelu_tanh(proj(c0))
            o_refs[1][0, :, c0:c0 + PROJ_CH] = proj(D + c0)
        else:
            o_refs[0][0, :, c0:c0 + PROJ_CH] = (proj(c0) * (HD ** -0.5)).astype(BF)
            o_refs[0][0, :, D + c0:D + c0 + PROJ_CH] = proj(D + c0).astype(BF)
            o_refs[0][0, :, 2 * D + c0:2 * D + c0 + PROJ_CH] = proj(2 * D + c0).astype(BF)


def _in_proj(x, mod, g, w, b, variant):
    bsz, t, _ = x.shape
    tm = _row_tile(t)
    n = w.shape[1]
    row = lambda width: pl.BlockSpec((1, tm, width), lambda bi, ti: (bi, ti, 0))
    if variant == "glu":
        out_shape = [jax.ShapeDtypeStruct((bsz, t, D), F32)]
        out_specs = [row(D)]
    elif variant == "lru":
        out_shape = [jax.ShapeDtypeStruct((bsz, t, D), F32)] * 2
        out_specs = [row(D), row(D)]
    else:
        out_shape = [jax.ShapeDtypeStruct((bsz, t, 3 * D), BF)]
        out_specs = [row(3 * D)]
    outs = pl.pallas_call(
        functools.partial(_in_proj_kernel, variant=variant),
        grid=(bsz, t // tm),
        in_specs=[
            row(D),
            pl.BlockSpec((1, SUBLANES, D), lambda bi, ti: (bi, 0, 0)),
            _const_spec((1, D)),
            _const_spec((D, n)),
            _const_spec((1, n)),
        ],
        out_specs=out_specs,
        out_shape=out_shape,
        compiler_params=_params(2),
    )(x, mod, g.reshape(1, D), w, b.reshape(1, n))
    return outs


CONV_HALO = 16
CONV_RC = 64
CONV_TM = 256
CONV_SPAN = (CONV_K - 1) // SUBLANES * SUBLANES


def _conv_kernel(x_ref, up_ref, uc_ref, un_ref, mod_ref, wdw_ref, bdw_ref, lng_ref, lnb_ref,
                 w2_ref, b2_ref, o_ref, buf_ref, sh_ref, acc_ref, *, tm, nt):
    t = pl.program_id(1)
    buf_ref[0:CONV_HALO, :] = jnp.where(t > 0, up_ref[0], 0.0)
    buf_ref[CONV_HALO:CONV_HALO + tm, :] = uc_ref[0]
    buf_ref[CONV_HALO + tm:2 * CONV_HALO + tm, :] = jnp.where(t < nt - 1, un_ref[0], 0.0)
    span = tm + CONV_SPAN
    for s in range(SUBLANES):
        for cb in range(D // LANES):
            lanes = slice(cb * LANES, (cb + 1) * LANES)
            sh_ref[s, cb] = buf_ref[s:s + span, lanes]
    off = CONV_HALO - CONV_K // 2

    def rows(rc, carry):
        base = pl.multiple_of(rc * CONV_RC, CONV_RC)
        for cb in range(D // LANES):
            lanes = slice(cb * LANES, (cb + 1) * LANES)
            acc = jnp.zeros((CONV_RC, LANES), F32)
            for k in range(CONV_K):
                s = (off + k) % SUBLANES
                q = off + k - s
                acc = acc + sh_ref[s, cb, pl.ds(base + q, CONV_RC), :] * wdw_ref[k:k + 1, lanes]
            acc_ref[pl.ds(base, CONV_RC), lanes] = acc + bdw_ref[:, lanes]
        return carry

    lax.fori_loop(0, tm // CONV_RC, rows, 0)

    v = acc_ref[...]
    mu = jnp.mean(v, axis=-1, keepdims=True)
    vc = v - mu
    var = jnp.mean(vc * vc, axis=-1, keepdims=True)
    y = vc * lax.rsqrt(var + EPS) * lng_ref[...] + lnb_ref[...]
    s = (y * _sigmoid(y)).astype(BF)
    y2 = _dot(s, w2_ref[...]) + b2_ref[...]
    o_ref[0] = x_ref[0] + mod_ref[0][2:3] * y2


def _conv_block(x, u, mod, w_dw, b_dw, ln_g, ln_b, w2, b2):
    bsz, t, _ = x.shape
    tm = CONV_TM
    nt = t // tm
    hb = tm // CONV_HALO
    row = pl.BlockSpec((1, tm, D), lambda bi, ti: (bi, ti, 0))
    return pl.pallas_call(
        functools.partial(_conv_kernel, tm=tm, nt=nt),
        grid=(bsz, nt),
        in_specs=[
            row,
            pl.BlockSpec((1, CONV_HALO, D), lambda bi, ti: (bi, jnp.maximum(ti * hb - 1, 0), 0)),
            row,
            pl.BlockSpec((1, CONV_HALO, D),
                         lambda bi, ti: (bi, jnp.minimum((ti + 1) * hb, t // CONV_HALO - 1), 0)),
            pl.BlockSpec((1, SUBLANES, D), lambda bi, ti: (bi, 0, 0)),
            _const_spec((CONV_K, D)),
            _const_spec((1, D)),
            _const_spec((1, D)),
            _const_spec((1, D)),
            _const_spec((D, D)),
            _const_spec((1, D)),
        ],
        out_specs=row,
        out_shape=jax.ShapeDtypeStruct((bsz, t, D), F32),
        scratch_shapes=[pltpu.VMEM((tm + 2 * CONV_HALO, D), F32),
                        pltpu.VMEM((SUBLANES, D // LANES, tm + CONV_SPAN, LANES), F32),
                        pltpu.VMEM((tm, D), F32)],
        compiler_params=_params(2),
    )(x, u, u, u, mod, w_dw, b_dw.reshape(1, D), ln_g.reshape(1, D), ln_b.reshape(1, D),
      w2, b2.reshape(1, D))


FFN_CH = 256


def _ffn_kernel(x_ref, mod_ref, g_ref, win_ref, wout_ref, fg_ref, o_ref, acc_ref, *, final):
    x = x_ref[0]
    mod = mod_ref[0]
    h = _normmod(x, g_ref[...], mod[4:5], mod[3:4]).astype(BF)
    for j in range(DFF // FFN_CH):
        c0 = j * FFN_CH
        u1 = _dot(h, win_ref[:, c0:c0 + FFN_CH])
        u2 = _dot(h, win_ref[:, DFF + c0:DFF + c0 + FFN_CH])
        a = ((u1 * _sigmoid(u1)) * u2).astype(BF)
        part = _dot(a, wout_ref[c0:c0 + FFN_CH, :])
        if j == 0:
            acc_ref[...] = part
        else:
            acc_ref[...] += part
    out = x + mod[5:6] * acc_ref[...]
    if final:
        ms = jnp.mean(out * out, axis=-1, keepdims=True)
        out = out * lax.rsqrt(ms + EPS) * fg_ref[...]
    o_ref[0] = out


def _ffn(x, mod, g, w_in, w_out, final_g=None):
    bsz, t, _ = x.shape
    tm = _row_tile(t)
    final = final_g is not None
    fg = (final_g if final else jnp.ones((D,), F32)).reshape(1, D)
    row = pl.BlockSpec((1, tm, D), lambda bi, ti: (bi, ti, 0))
    return pl.pallas_call(
        functools.partial(_ffn_kernel, final=final),
        grid=(bsz, t // tm),
        in_specs=[
            row,
            pl.BlockSpec((1, SUBLANES, D), lambda bi, ti: (bi, 0, 0)),
            _const_spec((1, D)),
            _const_spec((D, 2 * DFF)),
            _const_spec((DFF, D)),
            _const_spec((1, D)),
        ],
        out_specs=row,
        out_shape=jax.ShapeDtypeStruct((bsz, t, D), F32),
        scratch_shapes=[pltpu.VMEM((tm, D), F32)],
        compiler_params=_params(2),
    )(x, mod, g.reshape(1, D), w_in, w_out, fg)


LRU_HALO = 8


def _lru_scan_kernel(xp_ref, xc_ref, xn_ref, h0_ref, wc_ref, bc_ref, wg_ref, bg_ref, lam_ref,
                     hs_ref, hl_ref, xbuf_ref, pc_ref, bs_ref, h_ref, *, tm, nt, reverse):
    t = pl.program_id(1)
    tt = (nt - 1 - t) if reverse else t

    @pl.when(t == 0)
    def _():
        h_ref[...] = jnp.broadcast_to(h0_ref[0], (SUBLANES, D))

    xbuf_ref[0:LRU_HALO, :] = jnp.where(tt > 0, xp_ref[0], 0.0)
    xbuf_ref[LRU_HALO:LRU_HALO + tm, :] = xc_ref[0]
    xbuf_ref[LRU_HALO + tm:2 * LRU_HALO + tm, :] = jnp.where(tt < nt - 1, xn_ref[0], 0.0)
    off = LRU_HALO - RNN_CONV_K // 2
    nv = tm // SUBLANES
    sub = lax.broadcasted_iota(jnp.int32, (nv, SUBLANES, LANES), 1)

    for n in range(RNN_BLOCKS):
        lanes = slice(n * LANES, (n + 1) * LANES)
        xr = jnp.zeros((tm, LANES), F32)
        for k in range(RNN_CONV_K):
            xr = xr + xbuf_ref[off + k:off + k + tm, lanes] * wc_ref[k:k + 1, lanes]
        xr = xr + bc_ref[:, lanes]
        gts = _dot(xr.astype(BF), wg_ref[n]) + bg_ref[n]
        r = _sigmoid(gts[:, :LANES])
        gi = _sigmoid(gts[:, LANES:])
        nl = -lam_ref[:, lanes]
        sp = jnp.maximum(nl, 0.0) + jnp.log1p(jnp.exp(-jnp.abs(nl)))
        log_a = (-LRU_C) * r * sp
        a = jnp.exp(log_a)
        th = jnp.tanh(log_a)
        one_m_a2 = (-2.0 * th) / (1.0 - th)
        b = jnp.sqrt(one_m_a2) * (gi * xr)
        a3 = a.reshape(nv, SUBLANES, LANES)
        b3 = b.reshape(nv, SUBLANES, LANES)
        for sh in (1, 2, 4):
            if reverse:
                m = sub < SUBLANES - sh
                a_s = pltpu.roll(a3, SUBLANES - sh, 1)
                b_s = pltpu.roll(b3, SUBLANES - sh, 1)
            else:
                m = sub >= sh
                a_s = pltpu.roll(a3, sh, 1)
                b_s = pltpu.roll(b3, sh, 1)
            b3 = jnp.where(m, a3 * b_s + b3, b3)
            a3 = jnp.where(m, a3 * a_s, a3)
        pc_ref[:, lanes] = a3.reshape(tm, LANES)
        bs_ref[:, lanes] = b3.reshape(tm, LANES)

    edge = 0 if reverse else SUBLANES - 1

    def group(i, hcar):
        v = (nv - 1 - i) if reverse else i
        st = pl.multiple_of(v * SUBLANES, SUBLANES)
        res = bs_ref[pl.ds(st, SUBLANES), :] + pc_ref[pl.ds(st, SUBLANES), :] * hcar
        hs_ref[0, pl.ds(st, SUBLANES), :] = res
        return jnp.broadcast_to(res[edge:edge + 1, :], (SUBLANES, D))

    hfin = lax.fori_loop(0, nv, group, h_ref[...])
    h_ref[...] = hfin
    hl_ref[0] = hfin[0:1, :]


def _lru_scan(xpre, h0, w_conv, b_conv, wg, bg, lam, reverse):
    bsz, t, _ = xpre.shape
    tm = _row_tile(t)
    nt = t // tm
    hb = tm // LRU_HALO
    tidx = (lambda ti: nt - 1 - ti) if reverse else (lambda ti: ti)
    row = pl.BlockSpec((1, tm, D), lambda bi, ti: (bi, tidx(ti), 0))
    hs, hl = pl.pallas_call(
        functools.partial(_lru_scan_kernel, tm=tm, nt=nt, reverse=reverse),
        grid=(bsz, nt),
        in_specs=[
            pl.BlockSpec((1, LRU_HALO, D), lambda bi, ti: (bi, jnp.maximum(tidx(ti) * hb - 1, 0), 0)),
            row,
            pl.BlockSpec((1, LRU_HALO, D),
                         lambda bi, ti: (bi, jnp.minimum((tidx(ti) + 1) * hb, t // LRU_HALO - 1), 0)),
            pl.BlockSpec((1, 1, D), lambda bi, ti: (bi, 0, 0)),
            _const_spec((RNN_CONV_K, D)),
            _const_spec((1, D)),
            _const_spec((RNN_BLOCKS, RNN_BLOCK, 2 * RNN_BLOCK)),
            _const_spec((RNN_BLOCKS, 1, 2 * RNN_BLOCK)),
            _const_spec((1, D)),
        ],
        out_specs=[row, pl.BlockSpec((1, 1, D), lambda bi, ti: (bi, 0, 0))],
        out_shape=[jax.ShapeDtypeStruct((bsz, t, D), F32), jax.ShapeDtypeStruct((bsz, 1, D), F32)],
        scratch_shapes=[pltpu.VMEM((tm + 2 * LRU_HALO, D), F32), pltpu.VMEM((tm, D), F32),
                        pltpu.VMEM((tm, D), F32), pltpu.VMEM((SUBLANES, D), F32)],
        compiler_params=_params(2),
    )(xpre, xpre, xpre, h0, w_conv, b_conv.reshape(1, D), wg, bg, lam.reshape(1, D))
    return hs, hl


def _out_proj_kernel(x_ref, mod_ref, *refs, lru):
    if lru:
        hf_ref, hr_ref, gate_ref, w_ref, b_ref, o_ref = refs
        y = ((hf_ref[0] + hr_ref[0]) * gate_ref[0]).astype(BF)
    else:
        y_ref, w_ref, b_ref, o_ref = refs
        y = y_ref[0]
    o_ref[0] = x_ref[0] + mod_ref[0][2:3] * (_dot(y, w_ref[...]) + b_ref[...])


def _out_proj(x, mod, ys, w, b):
    bsz, t, _ = x.shape
    tm = _row_tile(t)
    row = pl.BlockSpec((1, tm, D), lambda bi, ti: (bi, ti, 0))
    return pl.pallas_call(
        functools.partial(_out_proj_kernel, lru=len(ys) == 3),
        grid=(bsz, t // tm),
        in_specs=[row, pl.BlockSpec((1, SUBLANES, D), lambda bi, ti: (bi, 0, 0))]
        + [row] * len(ys) + [_const_spec((D, D)), _const_spec((1, D))],
        out_specs=row,
        out_shape=jax.ShapeDtypeStruct((bsz, t, D), F32),
        compiler_params=_params(2),
    )(x, mod, *ys, w, b.reshape(1, D))


NA_QR = 4
NA_SR = 12
NA_HG = 4
NA_NQ = NA_QR * GRID_W
NA_NK = NA_SR * GRID_W
NA_ROWS = 8192 // GRID_W


def _na_kernel(q_ref, k_ref, v_ref, kc_ref, vc_ref, bias_ref, o_ref, *, nr):
    ri = pl.program_id(2)
    rs = jnp.clip(ri * NA_QR - NA_KH // 2, 0, NA_ROWS - NA_SR)
    st = pl.multiple_of(rs * GRID_W, GRID_W)
    outs = []
    for h in range(NA_HG):
        lanes = slice(h * HD, (h + 1) * HD)
        q = q_ref[0, :, lanes]
        ks = k_ref[0, pl.ds(st, NA_NK), lanes]
        vs = v_ref[0, pl.ds(st, NA_NK), lanes]
        s_loc = lax.dot_general(q, ks, (((1,), (1,)), ((), ())), preferred_element_type=F32)
        s_loc = s_loc + bias_ref[0, h]
        s_ctx = lax.dot_general(q, kc_ref[0, :, lanes], (((1,), (1,)), ((), ())),
                                preferred_element_type=F32)
        m = jnp.maximum(jnp.max(s_loc, axis=-1, keepdims=True), jnp.max(s_ctx, axis=-1, keepdims=True))
        p_loc = jnp.exp(s_loc - m)
        p_ctx = jnp.exp(s_ctx - m)
        l = jnp.sum(p_loc, axis=-1, keepdims=True) + jnp.sum(p_ctx, axis=-1, keepdims=True)
        o = _dot(p_loc.astype(BF), vs) + _dot(p_ctx.astype(BF), vc_ref[0, :, lanes])
        outs.append(o / l)
    o_ref[0] = jnp.concatenate(outs, axis=-1).astype(BF)


def _na_bias_tables(rpb):
    rows = NA_ROWS
    cols = np.arange(GRID_W)
    c0 = np.clip(cols - NA_KW // 2, 0, GRID_W - NA_KW)
    col_ok = (cols[None, :] >= c0[:, None]) & (cols[None, :] < c0[:, None] + NA_KW)
    col_off = cols[None, :] - cols[:, None] + (NA_KW - 1)
    onehot = ((col_off[..., None] == np.arange(2 * NA_KW - 1)) & col_ok[..., None]).astype(np.float32)
    cm = jnp.einsum("hab,ckb->hack", rpb, jnp.asarray(onehot), precision=lax.Precision.HIGHEST)
    cm = jnp.where(jnp.asarray(col_ok), cm, NEG)
    pad = NA_KH // 2
    cm = jnp.pad(cm, ((0, 0), (pad, pad), (0, 0), (0, 0)), constant_values=NEG)
    tabs = []
    for r in (0, NA_QR, rows - NA_QR):
        rs = min(max(r - NA_KH // 2, 0), rows - NA_SR)
        qr = r + np.arange(NA_QR)
        r0 = np.clip(qr - NA_KH // 2, 0, rows - NA_KH)
        kr = rs + np.arange(NA_SR)
        row_ok = (kr[None, :] >= r0[:, None]) & (kr[None, :] < r0[:, None] + NA_KH)
        g = jnp.stack([cm[:, pad + rs - int(q) + NA_KH - 1:pad + rs - int(q) + NA_KH - 1 + NA_SR]
                       for q in qr], axis=1)
        g = jnp.where(jnp.asarray(row_ok)[None, :, :, None, None], g, NEG)
        tabs.append(jnp.transpose(g, (0, 1, 3, 2, 4)).reshape(HEADS, NA_NQ, NA_NK))
    return jnp.stack(tabs)


def _na_attention(qkv, qkv_c, bias_tabs):
    bsz, s, _ = qkv.shape
    nr = NA_ROWS // NA_QR
    hw = NA_HG * HD
    nhg = HEADS // NA_HG

    def case(ri):
        return jnp.where(ri == 0, 0, jnp.where(ri == nr - 1, 2, 1))

    return pl.pallas_call(
        functools.partial(_na_kernel, nr=nr),
        grid=(bsz, nhg, nr),
        in_specs=[
            pl.BlockSpec((1, NA_NQ, hw), lambda bi, hg, ri: (bi, ri, hg)),
            pl.BlockSpec((1, s, hw), lambda bi, hg, ri: (bi, 0, nhg + hg)),
            pl.BlockSpec((1, s, hw), lambda bi, hg, ri: (bi, 0, 2 * nhg + hg)),
            pl.BlockSpec((1, CTX, hw), lambda bi, hg, ri: (bi, 0, nhg + hg)),
            pl.BlockSpec((1, CTX, hw), lambda bi, hg, ri: (bi, 0, 2 * nhg + hg)),
            pl.BlockSpec((1, NA_HG, NA_NQ, NA_NK), lambda bi, hg, ri: (case(ri), hg, 0, 0)),
        ],
        out_specs=pl.BlockSpec((1, NA_NQ, hw), lambda bi, hg, ri: (bi, ri, hg)),
        out_shape=jax.ShapeDtypeStruct((bsz, s, D), BF),
        compiler_params=_params(3),
    )(qkv, qkv, qkv, qkv_c, qkv_c, bias_tabs)


def kernel(x, c, ctx, c_ctx, w_mod, b_mod, norm_g, w_ffn_in, w_ffn_out, a_w_pw1, a_b_pw1, a_w_dw, a_b_dw, a_ln_g, a_ln_b, a_w_pw2, a_b_pw2, b_w_in, b_b_in, b_w_conv, b_b_conv, b_w_rg, b_b_rg, b_w_ig, b_b_ig, b_lam, b_w_out, b_b_out, c_w_qkv, c_b_qkv, c_rpb, c_w_o, c_b_o, final_g):
    bsz = x.shape[0]
    cond = jnp.concatenate([c, c_ctx[None], jnp.zeros((SUBLANES - bsz - 1, D), F32)], axis=0)
    mods = _mod_call(cond, w_mod, b_mod)

    def layer_mods(i):
        m = mods[i].reshape(SUBLANES, 6, D)
        pad = ((0, 0), (0, SUBLANES - 6), (0, 0))
        lat = jnp.pad(m[:bsz], pad)
        cx = jnp.pad(jnp.broadcast_to(m[bsz:bsz + 1], (bsz, 6, D)), pad)
        return lat, cx

    def conv_layer(xs, mod, i, j):
        (u,) = _in_proj(xs, mod, norm_g[i, 0], a_w_pw1[j].astype(BF), a_b_pw1[j], "glu")
        return _conv_block(xs, u, mod, a_w_dw[j], a_b_dw[j], a_ln_g[j], a_ln_b[j],
                           a_w_pw2[j].astype(BF), a_b_pw2[j])

    def ffn(xs, mod, i, final_g=None):
        return _ffn(xs, mod, norm_g[i, 1], w_ffn_in[i].astype(BF), w_ffn_out[i].astype(BF), final_g)

    cs = ctx

    lat, cx = layer_mods(0)
    x = ffn(conv_layer(x, lat, 0, 0), lat, 0)
    cs = ffn(conv_layer(cs, cx, 0, 0), cx, 0)

    lat, cx = layer_mods(1)
    w_in = b_w_in[0].astype(BF)
    w_out = b_w_out[0].astype(BF)
    gate_l, xpre_l = _in_proj(x, lat, norm_g[1, 0], w_in, b_b_in[0], "lru")
    gate_c, xpre_c = _in_proj(cs, cx, norm_g[1, 0], w_in, b_b_in[0], "lru")
    hs_l, hs_c = [], []
    for d, reverse in ((0, False), (1, True)):
        wg = jnp.concatenate([b_w_rg[0, d], b_w_ig[0, d]], axis=-1).astype(BF)
        bg = jnp.concatenate([b_b_rg[0, d].reshape(RNN_BLOCKS, 1, RNN_BLOCK),
                              b_b_ig[0, d].reshape(RNN_BLOCKS, 1, RNN_BLOCK)], axis=-1)
        h0 = jnp.zeros((bsz, 1, D), F32)
        hc, hc_last = _lru_scan(xpre_c, h0, b_w_conv[0], b_b_conv[0], wg, bg, b_lam[0, d], reverse)
        hl, _ = _lru_scan(xpre_l, hc_last, b_w_conv[0], b_b_conv[0], wg, bg, b_lam[0, d], reverse)
        hs_c.append(hc)
        hs_l.append(hl)
    x = ffn(_out_proj(x, lat, (hs_l[0], hs_l[1], gate_l), w_out, b_b_out[0]), lat, 1)
    cs = ffn(_out_proj(cs, cx, (hs_c[0], hs_c[1], gate_c), w_out, b_b_out[0]), cx, 1)

    lat, cx = layer_mods(2)
    w_qkv = c_w_qkv[0].astype(BF)
    (qkv,) = _in_proj(x, lat, norm_g[2, 0], w_qkv, c_b_qkv[0], "qkv")
    (qkv_c,) = _in_proj(cs, cx, norm_g[2, 0], w_qkv, c_b_qkv[0], "qkv")
    o = _na_attention(qkv, qkv_c, _na_bias_tables(c_rpb[0]))
    x = ffn(_out_proj(x, lat, (o,), c_w_o[0].astype(BF), c_b_o[0]), lat, 2)

    lat, _ = layer_mods(3)
    x = ffn(conv_layer(x, lat, 3, 1), lat, 3, final_g)
    return x
```

```python
import functools

import jax
import jax.numpy as jnp
import numpy as np
from jax import lax
from jax.experimental import pallas as pl
from jax.experimental.pallas import tpu as pltpu

D = 1024
DEPTH = 4
GRID_W = 64
CTX = 256
CONV_K = 31
RNN_BLOCKS = 8
RNN_BLOCK = 128
RNN_CONV_K = 4
LRU_C = 8.0
HEADS = 16
HD = 64
NA_KH = 8
NA_KW = 16
DFF = 2816
EPS = 1e-6

LANES = 128
SUBLANES = 8
VMEM_LIMIT = 56 * 1024 * 1024
NEG = -1e30

BF = jnp.bfloat16
F32 = jnp.float32


def _dot(a, b):
    return jnp.dot(a, b, preferred_element_type=F32)


def _dot_nt(a, b):
    return lax.dot_general(a, b, (((1,), (1,)), ((), ())), preferred_element_type=F32)


def _params(n_axes):
    return pltpu.CompilerParams(
        dimension_semantics=("arbitrary",) * n_axes, vmem_limit_bytes=VMEM_LIMIT)


def _const_spec(shape):
    nd = len(shape)
    return pl.BlockSpec(shape, lambda *_: (0,) * nd, pipeline_mode=pl.Buffered(1))


def _row_tile(t):
    return min(512, t)


def _normmod(x, g, sc, sh):
    ms = jnp.mean(x * x, axis=-1, keepdims=True)
    return (x * lax.rsqrt(ms + EPS) * g) * (1.0 + sc) + sh


def _sigmoid(x):
    return 0.5 * jnp.tanh(0.5 * x) + 0.5


def _gelu_tanh(x):
    return 0.5 * x * (1.0 + jnp.tanh(0.7978845608028654 * (x + 0.044715 * (x * x * x))))


MOD_TN = 1536


def _mod_kernel(c_ref, w_ref, b_ref, o_ref):
    cnd = c_ref[...]
    s = (cnd * _sigmoid(cnd)).astype(BF)
    o_ref[0] = _dot(s, w_ref[0].astype(BF)) + b_ref[0]


def _mod_call(cond, w_mod, b_mod):
    n = 6 * D
    return pl.pallas_call(
        _mod_kernel,
        grid=(DEPTH, n // MOD_TN),
        in_specs=[
            pl.BlockSpec((SUBLANES, D), lambda i, j: (0, 0)),
            pl.BlockSpec((1, D, MOD_TN), lambda i, j: (i, 0, j)),
            pl.BlockSpec((1, 1, MOD_TN), lambda i, j: (i, 0, j)),
        ],
        out_specs=pl.BlockSpec((1, SUBLANES, MOD_TN), lambda i, j: (i, 0, j)),
        out_shape=jax.ShapeDtypeStruct((DEPTH, SUBLANES, n), F32),
        compiler_params=_params(2),
    )(cond, w_mod, b_mod.reshape(DEPTH, 1, n))


PROJ_CH = 512


def _in_proj_kernel(x_ref, mod_ref, g_ref, w_ref, b_ref, *o_refs, variant):
    mod = mod_ref[0]
    h = _normmod(x_ref[0], g_ref[...], mod[1:2], mod[0:1]).astype(BF)

    def proj(c0):
        return _dot(h, w_ref[:, c0:c0 + PROJ_CH]) + b_ref[:, c0:c0 + PROJ_CH]

    for j in range(D // PROJ_CH):
        c0 = j * PROJ_CH
        if variant == "glu":
            o_refs[0][0, :, c0:c0 + PROJ_CH] = proj(c0) * _sigmoid(proj(D + c0))
        elif variant == "lru":
            o_refs[0][0, :, c0:c0 + PROJ_CH] = _gelu_tanh(proj(c0))
            o_refs[1][0, :, c0:c0 + PROJ_CH] = proj(D + c0)
        else:
            o_refs[0][0, :, c0:c0 + PROJ_CH] = (proj(c0) * (HD ** -0.5)).astype(BF)
            o_refs[0][0, :, D + c0:D + c0 + PROJ_CH] = proj(D + c0).astype(BF)
            o_refs[0][0, :, 2 * D + c0:2 * D + c0 + PROJ_CH] = proj(2 * D + c0).astype(BF)


def _in_proj(x, mod, g, w, b, variant):
    bsz, t, _ = x.shape
    tm = _row_tile(t)
    n = w.shape[1]
    row = lambda width: pl.BlockSpec((1, tm, width), lambda bi, ti: (bi, ti, 0))
    if variant == "glu":
        out_shape = [jax.ShapeDtypeStruct((bsz, t, D), F32)]
        out_specs = [row(D)]
    elif variant == "lru":
        out_shape = [jax.ShapeDtypeStruct((bsz, t, D), F32)] * 2
        out_specs = [row(D), row(D)]
    else:
        out_shape = [jax.ShapeDtypeStruct((bsz, t, 3 * D), BF)]
        out_specs = [row(3 * D)]
    outs = pl.pallas_call(
        functools.partial(_in_proj_kernel, variant=variant),
        grid=(bsz, t // tm),
        in_specs=[
            row(D),
            pl.BlockSpec((1, SUBLANES, D), lambda bi, ti: (bi, 0, 0)),
            _const_spec((1, D)),
            _const_spec((D, n)),
            _const_spec((1, n)),
        ],
        out_specs=out_specs,
        out_shape=out_shape,
        compiler_params=_params(2),
    )(x, mod, g.reshape(1, D), w, b.reshape(1, n))
    return outs


CONV_HALO = 16
CONV_RC = 64
CONV_TM = 256
CONV_SPAN = (CONV_K - 1) // SUBLANES * SUBLANES


def _conv_kernel(x_ref, up_ref, uc_ref, un_ref, mod_ref, wdw_ref, bdw_ref, lng_ref, lnb_ref,
                 w2_ref, b2_ref, o_ref, buf_ref, sh_ref, acc_ref, *, tm, nt):
    t = pl.program_id(1)
    buf_ref[0:CONV_HALO, :] = jnp.where(t > 0, up_ref[0], 0.0)
    buf_ref[CONV_HALO:CONV_HALO + tm, :] = uc_ref[0]
    buf_ref[CONV_HALO + tm:2 * CONV_HALO + tm, :] = jnp.where(t < nt - 1, un_ref[0], 0.0)
    span = tm + CONV_SPAN
    for s in range(SUBLANES):
        for cb in range(D // LANES):
            sh_ref[s, cb] = buf_ref[s:s + span, cb * LANES:(cb + 1) * LANES]
    off = CONV_HALO - CONV_K // 2

    def rows(rc, carry):
        base = pl.multiple_of(rc * CONV_RC, CONV_RC)
        for cb in range(D // LANES):
            lanes = slice(cb * LANES, (cb + 1) * LANES)
            acc = jnp.zeros((CONV_RC, LANES), F32)
            for k in range(CONV_K):
                s = (off + k) % SUBLANES
                q = off + k - s
                acc = acc + sh_ref[s, cb, pl.ds(base + q, CONV_RC), :] * wdw_ref[k:k + 1, lanes]
            acc_ref[pl.ds(base, CONV_RC), lanes] = acc + bdw_ref[:, lanes]
        return carry

    lax.fori_loop(0, tm // CONV_RC, rows, 0)

    v = acc_ref[...]
    mu = jnp.mean(v, axis=-1, keepdims=True)
    vc = v - mu
    var = jnp.mean(vc * vc, axis=-1, keepdims=True)
    y = vc * lax.rsqrt(var + EPS) * lng_ref[...] + lnb_ref[...]
    s_act = (y * _sigmoid(y)).astype(BF)
    y2 = _dot(s_act, w2_ref[...]) + b2_ref[...]
    o_ref[0] = x_ref[0] + mod_ref[0][2:3] * y2


def _conv_block(x, u, mod, w_dw, b_dw, ln_g, ln_b, w2, b2):
    bsz, t, _ = x.shape
    tm = CONV_TM
    nt = t // tm
    hb = tm // CONV_HALO
    row = pl.BlockSpec((1, tm, D), lambda bi, ti: (bi, ti, 0))
    vec = lambda a: a.reshape(1, D)
    return pl.pallas_call(
        functools.partial(_conv_kernel, tm=tm, nt=nt),
        grid=(bsz, nt),
        in_specs=[
            row,
            pl.BlockSpec((1, CONV_HALO, D), lambda bi, ti: (bi, jnp.maximum(ti * hb - 1, 0), 0)),
            row,
            pl.BlockSpec((1, CONV_HALO, D),
                         lambda bi, ti: (bi, jnp.minimum((ti + 1) * hb, t // CONV_HALO - 1), 0)),
            pl.BlockSpec((1, SUBLANES, D), lambda bi, ti: (bi, 0, 0)),
            _const_spec((CONV_K, D)),
            _const_spec((1, D)),
            _const_spec((1, D)),
            _const_spec((1, D)),
            _const_spec((D, D)),
            _const_spec((1, D)),
        ],
        out_specs=row,
        out_shape=jax.ShapeDtypeStruct((bsz, t, D), F32),
        scratch_shapes=[pltpu.VMEM((tm + 2 * CONV_HALO, D), F32),
                        pltpu.VMEM((SUBLANES, D // LANES, tm + CONV_SPAN, LANES), F32),
                        pltpu.VMEM((tm, D), F32)],
        compiler_params=_params(2),
    )(x, u, u, u, mod, w_dw, vec(b_dw), vec(ln_g), vec(ln_b), w2, vec(b2))


FFN_CH = 256


def _ffn_body(x, mod, g_ref, win_ref, wout_ref, fg_ref, acc_ref, final):
    h = _normmod(x, g_ref[...], mod[4:5], mod[3:4]).astype(BF)
    for j in range(DFF // FFN_CH):
        c0 = j * FFN_CH
        u1 = _dot(h, win_ref[:, c0:c0 + FFN_CH])
        u2 = _dot(h, win_ref[:, DFF + c0:DFF + c0 + FFN_CH])
        a = ((u1 * _sigmoid(u1)) * u2).astype(BF)
        part = _dot(a, wout_ref[c0:c0 + FFN_CH, :])
        if j == 0:
            acc_ref[...] = part
        else:
            acc_ref[...] += part
    out = x + mod[5:6] * acc_ref[...]
    if final:
        ms = jnp.mean(out * out, axis=-1, keepdims=True)
        out = out * lax.rsqrt(ms + EPS) * fg_ref[...]
    return out


def _ffn_kernel(x_ref, mod_ref, *refs, final, n_y):
    y_refs = refs[:n_y]
    if n_y:
        wp_ref, bp_ref = refs[n_y:n_y + 2]
        refs = refs[n_y + 2:]
    g_ref, win_ref, wout_ref, fg_ref, o_ref, acc_ref = refs[:6]
    mod = mod_ref[0]
    if n_y == 0:
        x = x_ref[0]
    else:
        x1_ref = refs[6]
        if n_y == 3:
            y = ((y_refs[0][0] + y_refs[1][0]) * y_refs[2][0]).astype(BF)
        else:
            y = y_refs[0][0]
        x1_ref[...] = x_ref[0] + mod[2:3] * (_dot(y, wp_ref[...]) + bp_ref[...])
        x = x1_ref[...]
    o_ref[0] = _ffn_body(x, mod, g_ref, win_ref, wout_ref, fg_ref, acc_ref, final)


def _ffn(x, mod, g, w_in, w_out, final_g=None, ys=(), w_p=None, b_p=None):
    bsz, t, _ = x.shape
    tm = _row_tile(t)
    final = final_g is not None
    fg = (final_g if final else jnp.ones((D,), F32)).reshape(1, D)
    row = pl.BlockSpec((1, tm, D), lambda bi, ti: (bi, ti, 0))
    n_y = len(ys)
    proj_specs = [row] * n_y + ([_const_spec((D, D)), _const_spec((1, D))] if n_y else [])
    proj_args = list(ys) + ([w_p, b_p.reshape(1, D)] if n_y else [])
    scratch = [pltpu.VMEM((tm, D), F32)] * (2 if n_y else 1)
    return pl.pallas_call(
        functools.partial(_ffn_kernel, final=final, n_y=n_y),
        grid=(bsz, t // tm),
        in_specs=[row, pl.BlockSpec((1, SUBLANES, D), lambda bi, ti: (bi, 0, 0))] + proj_specs + [
            _const_spec((1, D)),
            _const_spec((D, 2 * DFF)),
            _const_spec((DFF, D)),
            _const_spec((1, D)),
        ],
        out_specs=row,
        out_shape=jax.ShapeDtypeStruct((bsz, t, D), F32),
        scratch_shapes=scratch,
        compiler_params=_params(2),
    )(x, mod, *proj_args, g.reshape(1, D), w_in, w_out, fg)


LRU_HALO = 8


def _lru_scan_kernel(xp_ref, xc_ref, xn_ref, h0_ref, wc_ref, bc_ref, wg_ref, bg_ref, lam_ref,
                     hs_ref, hl_ref, xbuf_ref, pc_ref, bs_ref, h_ref, *, tm, nt, reverse):
    t = pl.program_id(1)
    tt = (nt - 1 - t) if reverse else t

    @pl.when(t == 0)
    def _():
        h_ref[...] = jnp.broadcast_to(h0_ref[0], (SUBLANES, D))

    xbuf_ref[0:LRU_HALO, :] = jnp.where(tt > 0, xp_ref[0], 0.0)
    xbuf_ref[LRU_HALO:LRU_HALO + tm, :] = xc_ref[0]
    xbuf_ref[LRU_HALO + tm:2 * LRU_HALO + tm, :] = jnp.where(tt < nt - 1, xn_ref[0], 0.0)
    off = LRU_HALO - RNN_CONV_K // 2
    nv = tm // SUBLANES
    sub = lax.broadcasted_iota(jnp.int32, (nv, SUBLANES, LANES), 1)

    for n in range(RNN_BLOCKS):
        lanes = slice(n * LANES, (n + 1) * LANES)
        xr = jnp.zeros((tm, LANES), F32)
        for k in range(RNN_CONV_K):
            xr = xr + xbuf_ref[off + k:off + k + tm, lanes] * wc_ref[k:k + 1, lanes]
        xr = xr + bc_ref[:, lanes]
        gts = _dot(xr.astype(BF), wg_ref[n]) + bg_ref[n]
        r = _sigmoid(gts[:, :LANES])
        gi = _sigmoid(gts[:, LANES:])
        nl = -lam_ref[:, lanes]
        sp = jnp.maximum(nl, 0.0) + jnp.log1p(jnp.exp(-jnp.abs(nl)))
        log_a = (-LRU_C) * r * sp
        a = jnp.exp(log_a)
        th = jnp.tanh(log_a)
        one_m_a2 = (-2.0 * th) / (1.0 - th)
        b = jnp.sqrt(one_m_a2) * (gi * xr)
        a3 = a.reshape(nv, SUBLANES, LANES)
        b3 = b.reshape(nv, SUBLANES, LANES)
        for sh in (1, 2, 4):
            if reverse:
                m = sub < SUBLANES - sh
                a_s = pltpu.roll(a3, SUBLANES - sh, 1)
                b_s = pltpu.roll(b3, SUBLANES - sh, 1)
            else:
                m = sub >= sh
                a_s = pltpu.roll(a3, sh, 1)
                b_s = pltpu.roll(b3, sh, 1)
            b3 = jnp.where(m, a3 * b_s + b3, b3)
            a3 = jnp.where(m, a3 * a_s, a3)
        pc_ref[:, lanes] = a3.reshape(tm, LANES)
        bs_ref[:, lanes] = b3.reshape(tm, LANES)

    edge = 0 if reverse else SUBLANES - 1

    def group(i, hcar):
        v = (nv - 1 - i) if reverse else i
        st = pl.multiple_of(v * SUBLANES, SUBLANES)
        res = bs_ref[pl.ds(st, SUBLANES), :] + pc_ref[pl.ds(st, SUBLANES), :] * hcar
        hs_ref[0, pl.ds(st, SUBLANES), :] = res
        return jnp.broadcast_to(res[edge:edge + 1, :], (SUBLANES, D))

    hfin = lax.fori_loop(0, nv, group, h_ref[...])
    h_ref[...] = hfin
    hl_ref[0] = hfin[0:1, :]


def _lru_scan(xpre, h0, w_conv, b_conv, wg, bg, lam, reverse):
    bsz, t, _ = xpre.shape
    tm = _row_tile(t)
    nt = t // tm
    hb = tm // LRU_HALO
    tidx = (lambda ti: nt - 1 - ti) if reverse else (lambda ti: ti)
    row = pl.BlockSpec((1, tm, D), lambda bi, ti: (bi, tidx(ti), 0))
    hs, hl = pl.pallas_call(
        functools.partial(_lru_scan_kernel, tm=tm, nt=nt, reverse=reverse),
        grid=(bsz, nt),
        in_specs=[
            pl.BlockSpec((1, LRU_HALO, D), lambda bi, ti: (bi, jnp.maximum(tidx(ti) * hb - 1, 0), 0)),
            row,
            pl.BlockSpec((1, LRU_HALO, D),
                         lambda bi, ti: (bi, jnp.minimum((tidx(ti) + 1) * hb, t // LRU_HALO - 1), 0)),
            pl.BlockSpec((1, 1, D), lambda bi, ti: (bi, 0, 0)),
            _const_spec((RNN_CONV_K, D)),
            _const_spec((1, D)),
            _const_spec((RNN_BLOCKS, RNN_BLOCK, 2 * RNN_BLOCK)),
            _const_spec((RNN_BLOCKS, 1, 2 * RNN_BLOCK)),
            _const_spec((1, D)),
        ],
        out_specs=[row, pl.BlockSpec((1, 1, D), lambda bi, ti: (bi, 0, 0))],
        out_shape=[jax.ShapeDtypeStruct((bsz, t, D), F32), jax.ShapeDtypeStruct((bsz, 1, D), F32)],
        scratch_shapes=[pltpu.VMEM((tm + 2 * LRU_HALO, D), F32), pltpu.VMEM((tm, D), F32),
                        pltpu.VMEM((tm, D), F32), pltpu.VMEM((SUBLANES, D), F32)],
        compiler_params=_params(2),
    )(xpre, xpre, xpre, h0, w_conv, b_conv.reshape(1, D), wg, bg, lam.reshape(1, D))
    return hs, hl


NA_QR = 4
NA_SR = 12
NA_HG = 4
NA_NQ = NA_QR * GRID_W
NA_NK = NA_SR * GRID_W
NA_ROWS = 8192 // GRID_W
NA_PAD = NA_KH // 2
NA_CASES = (0, NA_QR, NA_ROWS - NA_QR)


def _na_case_geometry(r):
    rs = min(max(r - NA_KH // 2, 0), NA_ROWS - NA_SR)
    qr = r + np.arange(NA_QR)
    r0 = np.clip(qr - NA_KH // 2, 0, NA_ROWS - NA_KH)
    kr = rs + np.arange(NA_SR)
    row_ok = (kr[None, :] >= r0[:, None]) & (kr[None, :] < r0[:, None] + NA_KH)
    a0 = [NA_PAD + rs - int(q) + NA_KH - 1 for q in qr]
    return row_ok, a0


def _na_table_kernel(cm2_ref, o_ref):
    lo = lax.broadcasted_iota(jnp.int32, (GRID_W, LANES), 1) < GRID_W
    for ci, r in enumerate(NA_CASES):
        row_ok, a0 = _na_case_geometry(r)
        for qi in range(NA_QR):
            for j in range(NA_SR // 2):
                blk = cm2_ref[0, a0[qi] + 2 * j]
                ok_l, ok_r = bool(row_ok[qi, 2 * j]), bool(row_ok[qi, 2 * j + 1])
                if not (ok_l or ok_r):
                    blk = jnp.full((GRID_W, LANES), NEG, F32)
                elif not ok_r:
                    blk = jnp.where(lo, blk, NEG)
                elif not ok_l:
                    blk = jnp.where(lo, NEG, blk)
                o_ref[ci, 0, qi * GRID_W:(qi + 1) * GRID_W, j * LANES:(j + 1) * LANES] = blk


def _na_bias_tables(rpb):
    cols = np.arange(GRID_W)
    c0 = np.clip(cols - NA_KW // 2, 0, GRID_W - NA_KW)
    col_ok = (cols[None, :] >= c0[:, None]) & (cols[None, :] < c0[:, None] + NA_KW)
    col_off = cols[None, :] - cols[:, None] + (NA_KW - 1)
    onehot = ((col_off[..., None] == np.arange(2 * NA_KW - 1)) & col_ok[..., None]).astype(np.float32)
    cm = jnp.einsum("hab,ckb->hack", rpb, jnp.asarray(onehot), precision=lax.Precision.HIGHEST)
    cm = jnp.where(jnp.asarray(col_ok), cm, NEG)
    cm = jnp.pad(cm, ((0, 0), (NA_PAD, NA_PAD), (0, 0), (0, 0)), constant_values=NEG)
    cm2 = jnp.concatenate([cm[:, :-1], cm[:, 1:]], axis=-1)
    na = cm2.shape[1]
    return pl.pallas_call(
        _na_table_kernel,
        grid=(HEADS,),
        in_specs=[pl.BlockSpec((1, na, GRID_W, LANES), lambda h: (h, 0, 0, 0))],
        out_specs=pl.BlockSpec((len(NA_CASES), 1, NA_NQ, NA_NK), lambda h: (0, h, 0, 0)),
        out_shape=jax.ShapeDtypeStruct((len(NA_CASES), HEADS, NA_NQ, NA_NK), F32),
        compiler_params=_params(1),
    )(cm2)


def _na_kernel(q_ref, k_ref, v_ref, kc_ref, vc_ref, bias_ref, o_ref):
    ri = pl.program_id(2)
    rs = jnp.clip(ri * NA_QR - NA_KH // 2, 0, NA_ROWS - NA_SR)
    st = pl.multiple_of(rs * GRID_W, GRID_W)
    lo = lax.broadcasted_iota(jnp.int32, (1, LANES), 1) < HD
    outs = []
    for hp in range(NA_HG // 2):
        lanes = slice(hp * LANES, (hp + 1) * LANES)
        qp = q_ref[0, :, lanes]
        kp = k_ref[0, pl.ds(st, NA_NK), lanes]
        kcp = kc_ref[0, :, lanes]
        vp = jnp.concatenate([v_ref[0, pl.ds(st, NA_NK), lanes], vc_ref[0, :, lanes]], axis=0)
        res = []
        for hh in range(2):
            own = lo if hh == 0 else jnp.logical_not(lo)
            qh = jnp.where(own, qp, jnp.zeros_like(qp))
            s = jnp.concatenate([_dot_nt(qh, kp) + bias_ref[0, 2 * hp + hh], _dot_nt(qh, kcp)], axis=-1)
            p = jnp.exp(s - jnp.max(s, axis=-1, keepdims=True)).astype(BF)
            res.append(_dot(p, jnp.where(own, vp, jnp.ones_like(vp))))
        num = jnp.where(lo, res[0], res[1])
        den = pltpu.roll(jnp.where(lo, res[1], res[0]), HD, 1)
        outs.append(num / den)
    o_ref[0] = jnp.concatenate(outs, axis=-1).astype(BF)


def _na_attention(qkv, qkv_c, bias_tabs):
    bsz, s, _ = qkv.shape
    nr = NA_ROWS // NA_QR
    hw = NA_HG * HD
    nhg = HEADS // NA_HG

    def case(ri):
        return jnp.where(ri == 0, 0, jnp.where(ri == nr - 1, 2, 1))

    return pl.pallas_call(
        _na_kernel,
        grid=(bsz, nhg, nr),
        in_specs=[
            pl.BlockSpec((1, NA_NQ, hw), lambda bi, hg, ri: (bi, ri, hg)),
            pl.BlockSpec((1, s, hw), lambda bi, hg, ri: (bi, 0, nhg + hg)),
            pl.BlockSpec((1, s, hw), lambda bi, hg, ri: (bi, 0, 2 * nhg + hg)),
            pl.BlockSpec((1, CTX, hw), lambda bi, hg, ri: (bi, 0, nhg + hg)),
            pl.BlockSpec((1, CTX, hw), lambda bi, hg, ri: (bi, 0, 2 * nhg + hg)),
            pl.BlockSpec((1, NA_HG, NA_NQ, NA_NK), lambda bi, hg, ri: (case(ri), hg, 0, 0)),
        ],
        out_specs=pl.BlockSpec((1, NA_NQ, hw), lambda bi, hg, ri: (bi, ri, hg)),
        out_shape=jax.ShapeDtypeStruct((bsz, s, D), BF),
        compiler_params=_params(3),
    )(qkv, qkv, qkv, qkv_c, qkv_c, bias_tabs)


def kernel(x, c, ctx, c_ctx, w_mod, b_mod, norm_g, w_ffn_in, w_ffn_out, a_w_pw1, a_b_pw1, a_w_dw, a_b_dw, a_ln_g, a_ln_b, a_w_pw2, a_b_pw2, b_w_in, b_b_in, b_w_conv, b_b_conv, b_w_rg, b_b_rg, b_w_ig, b_b_ig, b_lam, b_w_out, b_b_out, c_w_qkv, c_b_qkv, c_rpb, c_w_o, c_b_o, final_g):
    bsz = x.shape[0]
    cond = jnp.concatenate([c, c_ctx[None], jnp.zeros((SUBLANES - bsz - 1, D), F32)], axis=0)
    mods = _mod_call(cond, w_mod, b_mod)

    def layer_mods(i):
        m = mods[i].reshape(SUBLANES, 6, D)
        pad = ((0, 0), (0, SUBLANES - 6), (0, 0))
        lat = jnp.pad(m[:bsz], pad)
        cx = jnp.pad(jnp.broadcast_to(m[bsz:bsz + 1], (bsz, 6, D)), pad)
        return lat, cx

    def conv_layer(xs, mod, i, j):
        (u,) = _in_proj(xs, mod, norm_g[i, 0], a_w_pw1[j].astype(BF), a_b_pw1[j], "glu")
        return _conv_block(xs, u, mod, a_w_dw[j], a_b_dw[j], a_ln_g[j], a_ln_b[j],
                           a_w_pw2[j].astype(BF), a_b_pw2[j])

    def ffn(xs, mod, i, final_g=None, **proj):
        return _ffn(xs, mod, norm_g[i, 1], w_ffn_in[i].astype(BF), w_ffn_out[i].astype(BF),
                    final_g, **proj)

    cs = ctx

    lat, cx = layer_mods(0)
    x = ffn(conv_layer(x, lat, 0, 0), lat, 0)
    cs = ffn(conv_layer(cs, cx, 0, 0), cx, 0)

    lat, cx = layer_mods(1)
    w_in = b_w_in[0].astype(BF)
    w_out = b_w_out[0].astype(BF)
    gate_l, xpre_l = _in_proj(x, lat, norm_g[1, 0], w_in, b_b_in[0], "lru")
    gate_c, xpre_c = _in_proj(cs, cx, norm_g[1, 0], w_in, b_b_in[0], "lru")
    hs_l, hs_c = [], []
    for d, reverse in ((0, False), (1, True)):
        wg = jnp.concatenate([b_w_rg[0, d], b_w_ig[0, d]], axis=-1).astype(BF)
        bg = jnp.concatenate([b_b_rg[0, d].reshape(RNN_BLOCKS, 1, RNN_BLOCK),
                              b_b_ig[0, d].reshape(RNN_BLOCKS, 1, RNN_BLOCK)], axis=-1)
        h0 = jnp.zeros((bsz, 1, D), F32)
        hc, hc_last = _lru_scan(xpre_c, h0, b_w_conv[0], b_b_conv[0], wg, bg, b_lam[0, d], reverse)
        hl, _ = _lru_scan(xpre_l, hc_last, b_w_conv[0], b_b_conv[0], wg, bg, b_lam[0, d], reverse)
        hs_c.append(hc)
        hs_l.append(hl)
    x = ffn(x, lat, 1, ys=(hs_l[0], hs_l[1], gate_l), w_p=w_out, b_p=b_b_out[0])
    cs = ffn(cs, cx, 1, ys=(hs_c[0], hs_c[1], gate_c), w_p=w_out, b_p=b_b_out[0])

    lat, cx = layer_mods(2)
    w_qkv = c_w_qkv[0].astype(BF)
    (qkv,) = _in_proj(x, lat, norm_g[2, 0], w_qkv, c_b_qkv[0], "qkv")
    (qkv_c,) = _in_proj(cs, cx, norm_g[2, 0], w_qkv, c_b_qkv[0], "qkv")
    o = _na_attention(qkv, qkv_c, _na_bias_tables(c_rpb[0]))
    x = ffn(x, lat, 2, ys=(o,), w_p=c_w_o[0].astype(BF), b_p=c_b_o[0])

    lat, _ = layer_mods(3)
    return ffn(conv_layer(x, lat, 3, 1), lat, 3, final_g)
```

```python
import functools

import jax
import jax.numpy as jnp
import numpy as np
from jax import lax
from jax.experimental import pallas as pl
from jax.experimental.pallas import tpu as pltpu

D = 1024
DEPTH = 4
GRID_W = 64
CTX = 256
CONV_K = 31
RNN_BLOCKS = 8
RNN_BLOCK = 128
RNN_CONV_K = 4
LRU_C = 8.0
HEADS = 16
HD = 64
NA_KH = 8
NA_KW = 16
DFF = 2816
EPS = 1e-6

LANES = 128
SUBLANES = 8
VMEM_LIMIT = 56 * 1024 * 1024
NEG = -1e30

BF = jnp.bfloat16
F32 = jnp.float32


def _dot(a, b):
    return jnp.dot(a, b, preferred_element_type=F32)


def _dot_nt(a, b):
    return lax.dot_general(a, b, (((1,), (1,)), ((), ())), preferred_element_type=F32)


def _params(n_axes):
    return pltpu.CompilerParams(
        dimension_semantics=("arbitrary",) * n_axes, vmem_limit_bytes=VMEM_LIMIT)


def _const_spec(shape):
    nd = len(shape)
    return pl.BlockSpec(shape, lambda *_: (0,) * nd, pipeline_mode=pl.Buffered(1))


def _row_tile(t):
    return min(512, t)


def _normmod(x, g, sc, sh):
    ms = jnp.mean(x * x, axis=-1, keepdims=True)
    return (x * lax.rsqrt(ms + EPS) * g) * (1.0 + sc) + sh


def _sigmoid(x):
    return 0.5 * jnp.tanh(0.5 * x) + 0.5


def _gelu_tanh(x):
    return 0.5 * x * (1.0 + jnp.tanh(0.7978845608028654 * (x + 0.044715 * (x * x * x))))


MOD_TN = 1536


def _mod_kernel(c_ref, w_ref, b_ref, o_ref):
    cnd = c_ref[...]
    s = (cnd * _sigmoid(cnd)).astype(BF)
    o_ref[0] = _dot(s, w_ref[0].astype(BF)) + b_ref[0]


def _mod_call(cond, w_mod, b_mod):
    n = 6 * D
    return pl.pallas_call(
        _mod_kernel,
        grid=(DEPTH, n // MOD_TN),
        in_specs=[
            pl.BlockSpec((SUBLANES, D), lambda i, j: (0, 0)),
            pl.BlockSpec((1, D, MOD_TN), lambda i, j: (i, 0, j)),
            pl.BlockSpec((1, 1, MOD_TN), lambda i, j: (i, 0, j)),
        ],
        out_specs=pl.BlockSpec((1, SUBLANES, MOD_TN), lambda i, j: (i, 0, j)),
        out_shape=jax.ShapeDtypeStruct((DEPTH, SUBLANES, n), F32),
        compiler_params=_params(2),
    )(cond, w_mod, b_mod.reshape(DEPTH, 1, n))


PROJ_CH = 512


def _in_proj_kernel(x_ref, mod_ref, g_ref, w_ref, b_ref, o_ref, *, variant):
    mod = mod_ref[0]
    h = _normmod(x_ref[0], g_ref[...], mod[1:2], mod[0:1]).astype(BF)

    def proj(c0):
        return _dot(h, w_ref[:, c0:c0 + PROJ_CH]) + b_ref[:, c0:c0 + PROJ_CH]

    for j in range(D // PROJ_CH):
        c0 = j * PROJ_CH
        if variant == "glu":
            o_ref[0, :, c0:c0 + PROJ_CH] = proj(c0) * _sigmoid(proj(D + c0))
        else:
            o_ref[0, :, c0:c0 + PROJ_CH] = (proj(c0) * (HD ** -0.5)).astype(BF)
            o_ref[0, :, D + c0:D + c0 + PROJ_CH] = proj(D + c0).astype(BF)
            o_ref[0, :, 2 * D + c0:2 * D + c0 + PROJ_CH] = proj(2 * D + c0).astype(BF)


def _in_proj(x, mod, g, w, b, variant):
    bsz, t, _ = x.shape
    tm = _row_tile(t)
    n = w.shape[1]
    row = lambda width: pl.BlockSpec((1, tm, width), lambda bi, ti: (bi, ti, 0))
    if variant == "glu":
        out_shape, out_spec = jax.ShapeDtypeStruct((bsz, t, D), F32), row(D)
    else:
        out_shape, out_spec = jax.ShapeDtypeStruct((bsz, t, 3 * D), BF), row(3 * D)
    return pl.pallas_call(
        functools.partial(_in_proj_kernel, variant=variant),
        grid=(bsz, t // tm),
        in_specs=[
            row(D),
            pl.BlockSpec((1, SUBLANES, D), lambda bi, ti: (bi, 0, 0)),
            _const_spec((1, D)),
            _const_spec((D, n)),
            _const_spec((1, n)),
        ],
        out_specs=out_spec,
        out_shape=out_shape,
        compiler_params=_params(2),
    )(x, mod, g.reshape(1, D), w, b.reshape(1, n))


CONV_HALO = 16
CONV_RC = 64
CONV_TM = 256
CONV_SPAN = (CONV_K - 1) // SUBLANES * SUBLANES


def _conv_kernel(x_ref, up_ref, uc_ref, un_ref, mod_ref, wdw_ref, bdw_ref, lng_ref, lnb_ref,
                 w2_ref, b2_ref, o_ref, buf_ref, sh_ref, acc_ref, *, tm, nt):
    t = pl.program_id(1)
    buf_ref[0:CONV_HALO, :] = jnp.where(t > 0, up_ref[0], 0.0)
    buf_ref[CONV_HALO:CONV_HALO + tm, :] = uc_ref[0]
    buf_ref[CONV_HALO + tm:2 * CONV_HALO + tm, :] = jnp.where(t < nt - 1, un_ref[0], 0.0)
    span = tm + CONV_SPAN
    for s in range(SUBLANES):
        for cb in range(D // LANES):
            sh_ref[s, cb] = buf_ref[s:s + span, cb * LANES:(cb + 1) * LANES]
    off = CONV_HALO - CONV_K // 2

    def rows(rc, carry):
        base = pl.multiple_of(rc * CONV_RC, CONV_RC)
        for cb in range(D // LANES):
            lanes = slice(cb * LANES, (cb + 1) * LANES)
            acc = jnp.zeros((CONV_RC, LANES), F32)
            for k in range(CONV_K):
                s = (off + k) % SUBLANES
                q = off + k - s
                acc = acc + sh_ref[s, cb, pl.ds(base + q, CONV_RC), :] * wdw_ref[k:k + 1, lanes]
            acc_ref[pl.ds(base, CONV_RC), lanes] = acc + bdw_ref[:, lanes]
        return carry

    lax.fori_loop(0, tm // CONV_RC, rows, 0)

    v = acc_ref[...]
    mu = jnp.mean(v, axis=-1, keepdims=True)
    vc = v - mu
    var = jnp.mean(vc * vc, axis=-1, keepdims=True)
    y = vc * lax.rsqrt(var + EPS) * lng_ref[...] + lnb_ref[...]
    s_act = (y * _sigmoid(y)).astype(BF)
    y2 = _dot(s_act, w2_ref[...]) + b2_ref[...]
    o_ref[0] = x_ref[0] + mod_ref[0][2:3] * y2


def _conv_block(x, u, mod, w_dw, b_dw, ln_g, ln_b, w2, b2):
    bsz, t, _ = x.shape
    tm = CONV_TM
    nt = t // tm
    hb = tm // CONV_HALO
    row = pl.BlockSpec((1, tm, D), lambda bi, ti: (bi, ti, 0))
    vec = lambda a: a.reshape(1, D)
    return pl.pallas_call(
        functools.partial(_conv_kernel, tm=tm, nt=nt),
        grid=(bsz, nt),
        in_specs=[
            row,
            pl.BlockSpec((1, CONV_HALO, D), lambda bi, ti: (bi, jnp.maximum(ti * hb - 1, 0), 0)),
            row,
            pl.BlockSpec((1, CONV_HALO, D),
                         lambda bi, ti: (bi, jnp.minimum((ti + 1) * hb, t // CONV_HALO - 1), 0)),
            pl.BlockSpec((1, SUBLANES, D), lambda bi, ti: (bi, 0, 0)),
            _const_spec((CONV_K, D)),
            _const_spec((1, D)),
            _const_spec((1, D)),
            _const_spec((1, D)),
            _const_spec((D, D)),
            _const_spec((1, D)),
        ],
        out_specs=row,
        out_shape=jax.ShapeDtypeStruct((bsz, t, D), F32),
        scratch_shapes=[pltpu.VMEM((tm + 2 * CONV_HALO, D), F32),
                        pltpu.VMEM((SUBLANES, D // LANES, tm + CONV_SPAN, LANES), F32),
                        pltpu.VMEM((tm, D), F32)],
        compiler_params=_params(2),
    )(x, u, u, u, mod, w_dw, vec(b_dw), vec(ln_g), vec(ln_b), w2, vec(b2))


FFN_CH = 256


def _ffn_body(x, mod, g_ref, win_ref, wout_ref, fg_ref, acc_ref, final):
    h = _normmod(x, g_ref[...], mod[4:5], mod[3:4]).astype(BF)
    for j in range(DFF // FFN_CH):
        c0 = j * FFN_CH
        u1 = _dot(h, win_ref[:, c0:c0 + FFN_CH])
        u2 = _dot(h, win_ref[:, DFF + c0:DFF + c0 + FFN_CH])
        a = ((u1 * _sigmoid(u1)) * u2).astype(BF)
        part = _dot(a, wout_ref[c0:c0 + FFN_CH, :])
        if j == 0:
            acc_ref[...] = part
        else:
            acc_ref[...] += part
    out = x + mod[5:6] * acc_ref[...]
    if final:
        ms = jnp.mean(out * out, axis=-1, keepdims=True)
        out = out * lax.rsqrt(ms + EPS) * fg_ref[...]
    return out


def _ffn_kernel(x_ref, mod_ref, *refs, final, n_y):
    y_refs = refs[:n_y]
    if n_y:
        wp_ref, bp_ref = refs[n_y:n_y + 2]
        refs = refs[n_y + 2:]
    g_ref, win_ref, wout_ref, fg_ref, o_ref, acc_ref = refs[:6]
    mod = mod_ref[0]
    if n_y == 0:
        x = x_ref[0]
    else:
        x1_ref = refs[6]
        if n_y == 3:
            y = ((y_refs[0][0] + y_refs[1][0]) * y_refs[2][0]).astype(BF)
        else:
            y = y_refs[0][0]
        x1_ref[...] = x_ref[0] + mod[2:3] * (_dot(y, wp_ref[...]) + bp_ref[...])
        x = x1_ref[...]
    o_ref[0] = _ffn_body(x, mod, g_ref, win_ref, wout_ref, fg_ref, acc_ref, final)


def _ffn(x, mod, g, w_in, w_out, layer, final_g=None, ys=(), w_p=None, b_p=None):
    bsz, t, _ = x.shape
    tm = min(1024, t) if not ys else _row_tile(t)
    final = final_g is not None
    fg = (final_g if final else jnp.ones((D,), F32)).reshape(1, D)
    row = pl.BlockSpec((1, tm, D), lambda bi, ti: (bi, ti, 0))
    layer_spec = lambda r, c: pl.BlockSpec((None, r, c), lambda *_: (layer, 0, 0),
                                           pipeline_mode=pl.Buffered(1))
    n_y = len(ys)
    proj_specs = [row] * n_y + ([_const_spec((D, D)), _const_spec((1, D))] if n_y else [])
    proj_args = list(ys) + ([w_p, b_p.reshape(1, D)] if n_y else [])
    scratch = [pltpu.VMEM((tm, D), F32)] * (2 if n_y else 1)
    return pl.pallas_call(
        functools.partial(_ffn_kernel, final=final, n_y=n_y),
        grid=(bsz, t // tm),
        in_specs=[row, pl.BlockSpec((1, SUBLANES, D), lambda bi, ti: (bi, 0, 0))] + proj_specs + [
            _const_spec((1, D)),
            layer_spec(D, 2 * DFF),
            layer_spec(DFF, D),
            _const_spec((1, D)),
        ],
        out_specs=row,
        out_shape=jax.ShapeDtypeStruct((bsz, t, D), F32),
        scratch_shapes=scratch,
        compiler_params=_params(2),
    )(x, mod, *proj_args, g.reshape(1, D), w_in, w_out, fg)


LRU_HALO = 16


def _lru_in_kernel(xp_ref, x_ref, xn_ref, mod_ref, g_ref, w_ref, b_ref, wc_ref, bc_ref,
                   gate_ref, xr_ref, h_ref, u_ref, *, tm, nt):
    t = pl.program_id(1)
    mod = mod_ref[0]
    g, sc, sh = g_ref[...], mod[1:2], mod[0:1]
    h_ref[0:LRU_HALO, :] = _normmod(xp_ref[0], g, sc, sh).astype(BF)
    h_ref[LRU_HALO:LRU_HALO + tm, :] = _normmod(x_ref[0], g, sc, sh).astype(BF)
    h_ref[LRU_HALO + tm:2 * LRU_HALO + tm, :] = _normmod(xn_ref[0], g, sc, sh).astype(BF)
    for j in range(D // PROJ_CH):
        c0 = j * PROJ_CH
        cols = slice(c0, c0 + PROJ_CH)
        gate_ref[0, :, cols] = _gelu_tanh(
            _dot(h_ref[LRU_HALO:LRU_HALO + tm, :], w_ref[:, cols]) + b_ref[:, cols])
        u = _dot(h_ref[...], w_ref[:, D + c0:D + c0 + PROJ_CH]) + b_ref[:, D + c0:D + c0 + PROJ_CH]
        u_ref[0:LRU_HALO, cols] = jnp.where(t > 0, u[0:LRU_HALO], 0.0)
        u_ref[LRU_HALO:LRU_HALO + tm, cols] = u[LRU_HALO:LRU_HALO + tm]
        u_ref[LRU_HALO + tm:, cols] = jnp.where(t < nt - 1, u[LRU_HALO + tm:], 0.0)
    off = LRU_HALO - RNN_CONV_K // 2
    for n in range(D // LANES):
        lanes = slice(n * LANES, (n + 1) * LANES)
        xr = jnp.zeros((tm, LANES), F32)
        for k in range(RNN_CONV_K):
            xr = xr + u_ref[off + k:off + k + tm, lanes] * wc_ref[k:k + 1, lanes]
        xr_ref[0, :, lanes] = xr + bc_ref[:, lanes]


def _lru_in(x, mod, g, w, b, w_conv, b_conv):
    bsz, t, _ = x.shape
    tm = _row_tile(t)
    nt = t // tm
    hb = tm // LRU_HALO
    row = pl.BlockSpec((1, tm, D), lambda bi, ti: (bi, ti, 0))
    return pl.pallas_call(
        functools.partial(_lru_in_kernel, tm=tm, nt=nt),
        grid=(bsz, nt),
        in_specs=[
            pl.BlockSpec((1, LRU_HALO, D), lambda bi, ti: (bi, jnp.maximum(ti * hb - 1, 0), 0)),
            row,
            pl.BlockSpec((1, LRU_HALO, D),
                         lambda bi, ti: (bi, jnp.minimum((ti + 1) * hb, t // LRU_HALO - 1), 0)),
            pl.BlockSpec((1, SUBLANES, D), lambda bi, ti: (bi, 0, 0)),
            _const_spec((1, D)),
            _const_spec((D, 2 * D)),
            _const_spec((1, 2 * D)),
            _const_spec((RNN_CONV_K, D)),
            _const_spec((1, D)),
        ],
        out_specs=[row, row],
        out_shape=[jax.ShapeDtypeStruct((bsz, t, D), F32)] * 2,
        scratch_shapes=[pltpu.VMEM((tm + 2 * LRU_HALO, D), BF),
                        pltpu.VMEM((tm + 2 * LRU_HALO, D), F32)],
        compiler_params=_params(2),
    )(x, x, x, mod, g.reshape(1, D), w, b.reshape(1, 2 * D), w_conv, b_conv.reshape(1, D))


def _lru_scan_kernel(xr_ref, h0_ref, wg_ref, bg_ref, lam_ref, hs_ref, hl_ref,
                     pc_ref, bs_ref, h_ref, *, tm, reverse):
    t = pl.program_id(1)

    @pl.when(t == 0)
    def _():
        h_ref[...] = jnp.broadcast_to(h0_ref[0], (SUBLANES, D))

    nv = tm // SUBLANES
    sub = lax.broadcasted_iota(jnp.int32, (nv, SUBLANES, LANES), 1)

    for n in range(RNN_BLOCKS):
        lanes = slice(n * LANES, (n + 1) * LANES)
        xr = xr_ref[0, :, lanes]
        gts = _dot(xr.astype(BF), wg_ref[n]) + bg_ref[n]
        r = _sigmoid(gts[:, :LANES])
        gi = _sigmoid(gts[:, LANES:])
        nl = -lam_ref[:, lanes]
        sp = jnp.maximum(nl, 0.0) + jnp.log1p(jnp.exp(-jnp.abs(nl)))
        log_a = (-LRU_C) * r * sp
        a = jnp.exp(log_a)
        th = jnp.tanh(log_a)
        one_m_a2 = (-2.0 * th) / (1.0 - th)
        root = jnp.where(one_m_a2 > 0.0, one_m_a2 * lax.rsqrt(one_m_a2), 0.0)
        b = root * (gi * xr)
        a3 = a.reshape(nv, SUBLANES, LANES)
        b3 = b.reshape(nv, SUBLANES, LANES)
        for sh in (1, 2, 4):
            if reverse:
                m = sub < SUBLANES - sh
                a_s = pltpu.roll(a3, SUBLANES - sh, 1)
                b_s = pltpu.roll(b3, SUBLANES - sh, 1)
            else:
                m = sub >= sh
                a_s = pltpu.roll(a3, sh, 1)
                b_s = pltpu.roll(b3, sh, 1)
            b3 = jnp.where(m, a3 * b_s + b3, b3)
            a3 = jnp.where(m, a3 * a_s, a3)
        pc_ref[:, lanes] = a3.reshape(tm, LANES)
        bs_ref[:, lanes] = b3.reshape(tm, LANES)

    edge = 0 if reverse else SUBLANES - 1

    def group(i, hcar):
        v = (nv - 1 - i) if reverse else i
        st = pl.multiple_of(v * SUBLANES, SUBLANES)
        res = bs_ref[pl.ds(st, SUBLANES), :] + pc_ref[pl.ds(st, SUBLANES), :] * hcar
        hs_ref[0, pl.ds(st, SUBLANES), :] = res
        return jnp.broadcast_to(res[edge:edge + 1, :], (SUBLANES, D))

    hfin = lax.fori_loop(0, nv, group, h_ref[...])
    h_ref[...] = hfin
    hl_ref[0] = hfin[0:1, :]


def _lru_scan(xr, h0, wg, bg, lam, reverse):
    bsz, t, _ = xr.shape
    tm = _row_tile(t)
    nt = t // tm
    tidx = (lambda ti: nt - 1 - ti) if reverse else (lambda ti: ti)
    row = pl.BlockSpec((1, tm, D), lambda bi, ti: (bi, tidx(ti), 0))
    hs, hl = pl.pallas_call(
        functools.partial(_lru_scan_kernel, tm=tm, reverse=reverse),
        grid=(bsz, nt),
        in_specs=[
            row,
            pl.BlockSpec((1, 1, D), lambda bi, ti: (bi, 0, 0)),
            _const_spec((RNN_BLOCKS, RNN_BLOCK, 2 * RNN_BLOCK)),
            _const_spec((RNN_BLOCKS, 1, 2 * RNN_BLOCK)),
            _const_spec((1, D)),
        ],
        out_specs=[row, pl.BlockSpec((1, 1, D), lambda bi, ti: (bi, 0, 0))],
        out_shape=[jax.ShapeDtypeStruct((bsz, t, D), F32), jax.ShapeDtypeStruct((bsz, 1, D), F32)],
        scratch_shapes=[pltpu.VMEM((tm, D), F32), pltpu.VMEM((tm, D), F32),
                        pltpu.VMEM((SUBLANES, D), F32)],
        compiler_params=_params(2),
    )(xr, h0, wg, bg, lam.reshape(1, D))
    return hs, hl


NA_QR = 4
NA_SR = 12
NA_HG = 4
NA_NQ = NA_QR * GRID_W
NA_QC = NA_NQ
NA_NK = NA_SR * GRID_W
NA_ROWS = 8192 // GRID_W
NA_PAD = NA_KH // 2
NA_CASES = (0, NA_QR, NA_ROWS - NA_QR)


def _na_case_geometry(r):
    rs = min(max(r - NA_KH // 2, 0), NA_ROWS - NA_SR)
    qr = r + np.arange(NA_QR)
    r0 = np.clip(qr - NA_KH // 2, 0, NA_ROWS - NA_KH)
    kr = rs + np.arange(NA_SR)
    row_ok = (kr[None, :] >= r0[:, None]) & (kr[None, :] < r0[:, None] + NA_KH)
    a0 = [NA_PAD + rs - int(q) + NA_KH - 1 for q in qr]
    return row_ok, a0


def _na_table_kernel(cm2_ref, o_ref):
    lo = lax.broadcasted_iota(jnp.int32, (GRID_W, LANES), 1) < GRID_W
    for ci, r in enumerate(NA_CASES):
        row_ok, a0 = _na_case_geometry(r)
        for qi in range(NA_QR):
            for j in range(NA_SR // 2):
                blk = cm2_ref[0, a0[qi] + 2 * j]
                ok_l, ok_r = bool(row_ok[qi, 2 * j]), bool(row_ok[qi, 2 * j + 1])
                if not (ok_l or ok_r):
                    blk = jnp.full((GRID_W, LANES), NEG, F32)
                elif not ok_r:
                    blk = jnp.where(lo, blk, NEG)
                elif not ok_l:
                    blk = jnp.where(lo, NEG, blk)
                o_ref[ci, 0, qi * GRID_W:(qi + 1) * GRID_W, j * LANES:(j + 1) * LANES] = blk


def _na_bias_tables(rpb):
    cols = np.arange(GRID_W)
    c0 = np.clip(cols - NA_KW // 2, 0, GRID_W - NA_KW)
    col_ok = (cols[None, :] >= c0[:, None]) & (cols[None, :] < c0[:, None] + NA_KW)
    col_off = cols[None, :] - cols[:, None] + (NA_KW - 1)
    onehot = ((col_off[..., None] == np.arange(2 * NA_KW - 1)) & col_ok[..., None]).astype(np.float32)
    cm = jnp.einsum("hab,ckb->hack", rpb, jnp.asarray(onehot), precision=lax.Precision.HIGHEST)
    cm = jnp.where(jnp.asarray(col_ok), cm, NEG)
    cm = jnp.pad(cm, ((0, 0), (NA_PAD, NA_PAD), (0, 0), (0, 0)), constant_values=NEG)
    cm2 = jnp.concatenate([cm[:, :-1], cm[:, 1:]], axis=-1)
    na = cm2.shape[1]
    return pl.pallas_call(
        _na_table_kernel,
        grid=(HEADS,),
        in_specs=[pl.BlockSpec((1, na, GRID_W, LANES), lambda h: (h, 0, 0, 0))],
        out_specs=pl.BlockSpec((len(NA_CASES), 1, NA_NQ, NA_NK), lambda h: (0, h, 0, 0)),
        out_shape=jax.ShapeDtypeStruct((len(NA_CASES), HEADS, NA_NQ, NA_NK), F32),
        compiler_params=_params(1),
    )(cm2)


def _na_kernel(q_ref, k_ref, v_ref, kc_ref, vc_ref, bias_ref, o_ref):
    ri = pl.program_id(2)
    rs = jnp.clip(ri * NA_QR - NA_KH // 2, 0, NA_ROWS - NA_SR)
    st = pl.multiple_of(rs * GRID_W, GRID_W)
    lo = lax.broadcasted_iota(jnp.int32, (1, LANES), 1) < HD
    for hp in range(NA_HG // 2):
        lanes = slice(hp * LANES, (hp + 1) * LANES)
        kp = k_ref[0, pl.ds(st, NA_NK), lanes]
        kcp = kc_ref[0, :, lanes]
        vp = jnp.concatenate([v_ref[0, pl.ds(st, NA_NK), lanes], vc_ref[0, :, lanes]], axis=0)
        for qc in range(NA_NQ // NA_QC):
            rows = slice(qc * NA_QC, (qc + 1) * NA_QC)
            qp = q_ref[0, rows, lanes]
            res = []
            for hh in range(2):
                own = lo if hh == 0 else jnp.logical_not(lo)
                qh = jnp.where(own, qp, jnp.zeros_like(qp))
                s = jnp.concatenate([_dot_nt(qh, kp) + bias_ref[0, 2 * hp + hh, rows, :],
                                     _dot_nt(qh, kcp)], axis=-1)
                p = jnp.exp(s - jnp.max(s, axis=-1, keepdims=True)).astype(BF)
                res.append(_dot(p, jnp.where(own, vp, jnp.ones_like(vp))))
            num = jnp.where(lo, res[0], res[1])
            den = pltpu.roll(jnp.where(lo, res[1], res[0]), HD, 1)
            o_ref[0, rows, lanes] = (num / den).astype(BF)


def _na_attention(qkv, qkv_c, bias_tabs):
    bsz, s, _ = qkv.shape
    nr = NA_ROWS // NA_QR
    hw = NA_HG * HD
    nhg = HEADS // NA_HG

    def case(ri):
        return jnp.where(ri == 0, 0, jnp.where(ri == nr - 1, 2, 1))

    return pl.pallas_call(
        _na_kernel,
        grid=(bsz, nhg, nr),
        in_specs=[
            pl.BlockSpec((1, NA_NQ, hw), lambda bi, hg, ri: (bi, ri, hg)),
            pl.BlockSpec((1, s, hw), lambda bi, hg, ri: (bi, 0, nhg + hg)),
            pl.BlockSpec((1, s, hw), lambda bi, hg, ri: (bi, 0, 2 * nhg + hg)),
            pl.BlockSpec((1, CTX, hw), lambda bi, hg, ri: (bi, 0, nhg + hg)),
            pl.BlockSpec((1, CTX, hw), lambda bi, hg, ri: (bi, 0, 2 * nhg + hg)),
            pl.BlockSpec((1, NA_HG, NA_NQ, NA_NK), lambda bi, hg, ri: (case(ri), hg, 0, 0)),
        ],
        out_specs=pl.BlockSpec((1, NA_NQ, hw), lambda bi, hg, ri: (bi, ri, hg)),
        out_shape=jax.ShapeDtypeStruct((bsz, s, D), BF),
        compiler_params=_params(3),
    )(qkv, qkv, qkv, qkv_c, qkv_c, bias_tabs)


def kernel(x, c, ctx, c_ctx, w_mod, b_mod, norm_g, w_ffn_in, w_ffn_out, a_w_pw1, a_b_pw1, a_w_dw, a_b_dw, a_ln_g, a_ln_b, a_w_pw2, a_b_pw2, b_w_in, b_b_in, b_w_conv, b_b_conv, b_w_rg, b_b_rg, b_w_ig, b_b_ig, b_lam, b_w_out, b_b_out, c_w_qkv, c_b_qkv, c_rpb, c_w_o, c_b_o, final_g):
    bsz = x.shape[0]
    cond = jnp.concatenate([c, c_ctx[None], jnp.zeros((SUBLANES - bsz - 1, D), F32)], axis=0)
    mods = _mod_call(cond, w_mod, b_mod)

    def layer_mods(i):
        m = mods[i].reshape(SUBLANES, 6, D)
        pad = ((0, 0), (0, SUBLANES - 6), (0, 0))
        lat = jnp.pad(m[:bsz], pad)
        cx = jnp.pad(jnp.broadcast_to(m[bsz:bsz + 1], (bsz, 6, D)), pad)
        return lat, cx

    def conv_layer(xs, mod, i, j):
        u = _in_proj(xs, mod, norm_g[i, 0], a_w_pw1[j].astype(BF), a_b_pw1[j], "glu")
        return _conv_block(xs, u, mod, a_w_dw[j], a_b_dw[j], a_ln_g[j], a_ln_b[j],
                           a_w_pw2[j].astype(BF), a_b_pw2[j])

    w_ffn_in_bf = w_ffn_in.astype(BF)
    w_ffn_out_bf = w_ffn_out.astype(BF)

    def ffn(xs, mod, i, final_g=None, **proj):
        return _ffn(xs, mod, norm_g[i, 1], w_ffn_in_bf, w_ffn_out_bf, i, final_g, **proj)

    cs = ctx

    lat, cx = layer_mods(0)
    x = ffn(conv_layer(x, lat, 0, 0), lat, 0)
    cs = ffn(conv_layer(cs, cx, 0, 0), cx, 0)

    lat, cx = layer_mods(1)
    w_in = b_w_in[0].astype(BF)
    w_out = b_w_out[0].astype(BF)
    gate_l, xr_l = _lru_in(x, lat, norm_g[1, 0], w_in, b_b_in[0], b_w_conv[0], b_b_conv[0])
    gate_c, xr_c = _lru_in(cs, cx, norm_g[1, 0], w_in, b_b_in[0], b_w_conv[0], b_b_conv[0])
    hs_l, hs_c = [], []
    for d, reverse in ((0, False), (1, True)):
        wg = jnp.concatenate([b_w_rg[0, d], b_w_ig[0, d]], axis=-1).astype(BF)
        bg = jnp.concatenate([b_b_rg[0, d].reshape(RNN_BLOCKS, 1, RNN_BLOCK),
                              b_b_ig[0, d].reshape(RNN_BLOCKS, 1, RNN_BLOCK)], axis=-1)
        h0 = jnp.zeros((bsz, 1, D), F32)
        hc, hc_last = _lru_scan(xr_c, h0, wg, bg, b_lam[0, d], reverse)
        hl, _ = _lru_scan(xr_l, hc_last, wg, bg, b_lam[0, d], reverse)
        hs_c.append(hc)
        hs_l.append(hl)
    x = ffn(x, lat, 1, ys=(hs_l[0], hs_l[1], gate_l), w_p=w_out, b_p=b_b_out[0])
    cs = ffn(cs, cx, 1, ys=(hs_c[0], hs_c[1], gate_c), w_p=w_out, b_p=b_b_out[0])

    lat, cx = layer_mods(2)
    w_qkv = c_w_qkv[0].astype(BF)
    qkv = _in_proj(x, lat, norm_g[2, 0], w_qkv, c_b_qkv[0], "qkv")
    qkv_c = _in_proj(cs, cx, norm_g[2, 0], w_qkv, c_b_qkv[0], "qkv")
    o = _na_attention(qkv, qkv_c, _na_bias_tables(c_rpb[0]))
    x = ffn(x, lat, 2, ys=(o,), w_p=c_w_o[0].astype(BF), b_p=c_b_o[0])

    lat, _ = layer_mods(3)
    return ffn(conv_layer(x, lat, 3, 1), lat, 3, final_g)
```

```python
import functools

import jax
import jax.numpy as jnp
import numpy as np
from jax import lax
from jax.experimental import pallas as pl
from jax.experimental.pallas import tpu as pltpu

D = 1024
DEPTH = 4
GRID_W = 64
CTX = 256
CONV_K = 31
RNN_BLOCKS = 8
RNN_BLOCK = 128
RNN_CONV_K = 4
LRU_C = 8.0
HEADS = 16
HD = 64
NA_KH = 8
NA_KW = 16
DFF = 2816
EPS = 1e-6

LANES = 128
SUBLANES = 8
VMEM_LIMIT = 56 * 1024 * 1024
NEG = -1e30

BF = jnp.bfloat16
F32 = jnp.float32


def _dot(a, b):
    return jnp.dot(a, b, preferred_element_type=F32)


def _dot_nt(a, b):
    return lax.dot_general(a, b, (((1,), (1,)), ((), ())), preferred_element_type=F32)


def _params(n_axes):
    return pltpu.CompilerParams(
        dimension_semantics=("arbitrary",) * n_axes, vmem_limit_bytes=VMEM_LIMIT)


def _const_spec(shape):
    nd = len(shape)
    return pl.BlockSpec(shape, lambda *_: (0,) * nd, pipeline_mode=pl.Buffered(1))


def _row_tile(t):
    return min(512, t)


def _normmod(x, g, sc, sh):
    ms = jnp.mean(x * x, axis=-1, keepdims=True)
    return (x * lax.rsqrt(ms + EPS) * g) * (1.0 + sc) + sh


def _sigmoid(x):
    return 0.5 * jnp.tanh(0.5 * x) + 0.5


def _gelu_tanh(x):
    return 0.5 * x * (1.0 + jnp.tanh(0.7978845608028654 * (x + 0.044715 * (x * x * x))))


MOD_TN = 1536


def _mod_kernel(c_ref, w_ref, b_ref, o_ref):
    cnd = c_ref[...]
    s = (cnd * _sigmoid(cnd)).astype(BF)
    o_ref[0] = _dot(s, w_ref[0].astype(BF)) + b_ref[0]


def _mod_call(cond, w_mod, b_mod):
    n = 6 * D
    return pl.pallas_call(
        _mod_kernel,
        grid=(DEPTH, n // MOD_TN),
        in_specs=[
            pl.BlockSpec((SUBLANES, D), lambda i, j: (0, 0)),
            pl.BlockSpec((1, D, MOD_TN), lambda i, j: (i, 0, j)),
            pl.BlockSpec((1, 1, MOD_TN), lambda i, j: (i, 0, j)),
        ],
        out_specs=pl.BlockSpec((1, SUBLANES, MOD_TN), lambda i, j: (i, 0, j)),
        out_shape=jax.ShapeDtypeStruct((DEPTH, SUBLANES, n), F32),
        compiler_params=_params(2),
    )(cond, w_mod, b_mod.reshape(DEPTH, 1, n))


PROJ_CH = 512


def _in_proj_kernel(x_ref, mod_ref, g_ref, w_ref, b_ref, o_ref, *, variant, tm):
    mod = mod_ref[0]
    rs = tm // 2
    for r in range(2):
        rows = slice(r * rs, (r + 1) * rs)
        h = _normmod(x_ref[0, rows, :], g_ref[...], mod[1:2], mod[0:1]).astype(BF)

        def proj(c0):
            return _dot(h, w_ref[:, c0:c0 + PROJ_CH]) + b_ref[:, c0:c0 + PROJ_CH]

        for j in range(D // PROJ_CH):
            c0 = j * PROJ_CH
            if variant == "glu":
                o_ref[0, rows, c0:c0 + PROJ_CH] = proj(c0) * _sigmoid(proj(D + c0))
            else:
                o_ref[0, rows, c0:c0 + PROJ_CH] = (proj(c0) * (HD ** -0.5)).astype(BF)
                o_ref[0, rows, D + c0:D + c0 + PROJ_CH] = proj(D + c0).astype(BF)
                o_ref[0, rows, 2 * D + c0:2 * D + c0 + PROJ_CH] = proj(2 * D + c0).astype(BF)


def _in_proj(x, mod, g, w, b, variant):
    bsz, t, _ = x.shape
    tm = _row_tile(t)
    n = w.shape[1]
    row = lambda width: pl.BlockSpec((1, tm, width), lambda bi, ti: (bi, ti, 0))
    if variant == "glu":
        out_shape, out_spec = jax.ShapeDtypeStruct((bsz, t, D), F32), row(D)
    else:
        out_shape, out_spec = jax.ShapeDtypeStruct((bsz, t, 3 * D), BF), row(3 * D)
    return pl.pallas_call(
        functools.partial(_in_proj_kernel, variant=variant, tm=tm),
        grid=(bsz, t // tm),
        in_specs=[
            row(D),
            pl.BlockSpec((1, SUBLANES, D), lambda bi, ti: (bi, 0, 0)),
            _const_spec((1, D)),
            _const_spec((D, n)),
            _const_spec((1, n)),
        ],
        out_specs=out_spec,
        out_shape=out_shape,
        compiler_params=_params(2),
    )(x, mod, g.reshape(1, D), w, b.reshape(1, n))


CONV_HALO = 16
CONV_RC = 64
CONV_TM = 256
CONV_SPAN = (CONV_K - 1) // SUBLANES * SUBLANES


def _conv_kernel(x_ref, up_ref, uc_ref, un_ref, mod_ref, wdw_ref, bdw_ref, lng_ref, lnb_ref,
                 w2_ref, b2_ref, o_ref, sh_ref, acc_ref, *, tm, nt):
    t = pl.program_id(1)
    for cb in range(D // LANES):
        lanes = slice(cb * LANES, (cb + 1) * LANES)
        sh_ref[0, cb, 0:CONV_HALO, :] = jnp.where(t > 0, up_ref[0, :, lanes], 0.0)
        sh_ref[0, cb, CONV_HALO:CONV_HALO + tm, :] = uc_ref[0, :, lanes]
        sh_ref[0, cb, CONV_HALO + tm:2 * CONV_HALO + tm, :] = jnp.where(
            t < nt - 1, un_ref[0, :, lanes], 0.0)
    span = tm + CONV_SPAN
    for s in range(1, SUBLANES):
        for cb in range(D // LANES):
            sh_ref[s, cb, 0:span, :] = sh_ref[0, cb, s:s + span, :]
    off = CONV_HALO - CONV_K // 2

    def rows(rc, carry):
        base = pl.multiple_of(rc * CONV_RC, CONV_RC)
        for cb in range(D // LANES):
            lanes = slice(cb * LANES, (cb + 1) * LANES)
            acc = jnp.zeros((CONV_RC, LANES), F32)
            for k in range(CONV_K):
                s = (off + k) % SUBLANES
                q = off + k - s
                acc = acc + sh_ref[s, cb, pl.ds(base + q, CONV_RC), :] * wdw_ref[k:k + 1, lanes]
            acc_ref[pl.ds(base, CONV_RC), lanes] = acc + bdw_ref[:, lanes]
        return carry

    lax.fori_loop(0, tm // CONV_RC, rows, 0)

    v = acc_ref[...]
    mu = jnp.mean(v, axis=-1, keepdims=True)
    vc = v - mu
    var = jnp.mean(vc * vc, axis=-1, keepdims=True)
    y = vc * lax.rsqrt(var + EPS) * lng_ref[...] + lnb_ref[...]
    s_act = (y * _sigmoid(y)).astype(BF)
    y2 = _dot(s_act, w2_ref[...]) + b2_ref[...]
    o_ref[0] = x_ref[0] + mod_ref[0][2:3] * y2


def _conv_block(x, u, mod, w_dw, b_dw, ln_g, ln_b, w2, b2):
    bsz, t, _ = x.shape
    tm = CONV_TM
    nt = t // tm
    hb = tm // CONV_HALO
    row = pl.BlockSpec((1, tm, D), lambda bi, ti: (bi, ti, 0))
    vec = lambda a: a.reshape(1, D)
    return pl.pallas_call(
        functools.partial(_conv_kernel, tm=tm, nt=nt),
        grid=(bsz, nt),
        in_specs=[
            row,
            pl.BlockSpec((1, CONV_HALO, D), lambda bi, ti: (bi, jnp.maximum(ti * hb - 1, 0), 0)),
            row,
            pl.BlockSpec((1, CONV_HALO, D),
                         lambda bi, ti: (bi, jnp.minimum((ti + 1) * hb, t // CONV_HALO - 1), 0)),
            pl.BlockSpec((1, SUBLANES, D), lambda bi, ti: (bi, 0, 0)),
            _const_spec((CONV_K, D)),
            _const_spec((1, D)),
            _const_spec((1, D)),
            _const_spec((1, D)),
            _const_spec((D, D)),
            _const_spec((1, D)),
        ],
        out_specs=row,
        out_shape=jax.ShapeDtypeStruct((bsz, t, D), F32),
        scratch_shapes=[pltpu.VMEM((SUBLANES, D // LANES, tm + 2 * CONV_HALO, LANES), F32),
                        pltpu.VMEM((tm, D), F32)],
        compiler_params=_params(2),
    )(x, u, u, u, mod, w_dw, vec(b_dw), vec(ln_g), vec(ln_b), w2, vec(b2))


FFN_CH = 256


def _ffn_body(x, mod, g_ref, win_ref, wout_ref, fg_ref, acc_ref, final):
    h = _normmod(x, g_ref[...], mod[4:5], mod[3:4]).astype(BF)
    for j in range(DFF // FFN_CH):
        c0 = j * FFN_CH
        u1 = _dot(h, win_ref[:, c0:c0 + FFN_CH])
        u2 = _dot(h, win_ref[:, DFF + c0:DFF + c0 + FFN_CH])
        a = ((u1 * _sigmoid(u1)) * u2).astype(BF)
        part = _dot(a, wout_ref[c0:c0 + FFN_CH, :])
        if j == 0:
            acc_ref[...] = part
        else:
            acc_ref[...] += part
    out = x + mod[5:6] * acc_ref[...]
    if final:
        ms = jnp.mean(out * out, axis=-1, keepdims=True)
        out = out * lax.rsqrt(ms + EPS) * fg_ref[...]
    return out


def _ffn_kernel(x_ref, mod_ref, *refs, final, n_y):
    y_refs = refs[:n_y]
    if n_y:
        wp_ref, bp_ref = refs[n_y:n_y + 2]
        refs = refs[n_y + 2:]
    g_ref, win_ref, wout_ref, fg_ref, o_ref, acc_ref = refs[:6]
    mod = mod_ref[0]
    if n_y == 0:
        x = x_ref[0]
    else:
        x1_ref = refs[6]
        if n_y == 3:
            y = ((y_refs[0][0] + y_refs[1][0]) * y_refs[2][0]).astype(BF)
        else:
            y = y_refs[0][0]
        x1_ref[...] = x_ref[0] + mod[2:3] * (_dot(y, wp_ref[...]) + bp_ref[...])
        x = x1_ref[...]
    o_ref[0] = _ffn_body(x, mod, g_ref, win_ref, wout_ref, fg_ref, acc_ref, final)


def _ffn(x, mod, g, w_in, w_out, layer, final_g=None, ys=(), w_p=None, b_p=None):
    bsz, t, _ = x.shape
    tm = min(1024, t) if not ys else _row_tile(t)
    final = final_g is not None
    fg = (final_g if final else jnp.ones((D,), F32)).reshape(1, D)
    row = pl.BlockSpec((1, tm, D), lambda bi, ti: (bi, ti, 0))
    layer_spec = lambda r, c: pl.BlockSpec((None, r, c), lambda *_: (layer, 0, 0),
                                           pipeline_mode=pl.Buffered(1))
    n_y = len(ys)
    proj_specs = [row] * n_y + ([_const_spec((D, D)), _const_spec((1, D))] if n_y else [])
    proj_args = list(ys) + ([w_p, b_p.reshape(1, D)] if n_y else [])
    scratch = [pltpu.VMEM((tm, D), F32)] * (2 if n_y else 1)
    return pl.pallas_call(
        functools.partial(_ffn_kernel, final=final, n_y=n_y),
        grid=(bsz, t // tm),
        in_specs=[row, pl.BlockSpec((1, SUBLANES, D), lambda bi, ti: (bi, 0, 0))] + proj_specs + [
            _const_spec((1, D)),
            layer_spec(D, 2 * DFF),
            layer_spec(DFF, D),
            _const_spec((1, D)),
        ],
        out_specs=row,
        out_shape=jax.ShapeDtypeStruct((bsz, t, D), F32),
        scratch_shapes=scratch,
        compiler_params=_params(2),
    )(x, mod, *proj_args, g.reshape(1, D), w_in, w_out, fg)


LRU_HALO = 16


def _lru_in_kernel(xp_ref, x_ref, xn_ref, mod_ref, g_ref, w_ref, b_ref, wc_ref, bc_ref,
                   gate_ref, xr_ref, h_ref, u_ref, *, tm, nt):
    t = pl.program_id(1)
    mod = mod_ref[0]
    g, sc, sh = g_ref[...], mod[1:2], mod[0:1]
    h_ref[0:LRU_HALO, :] = _normmod(xp_ref[0], g, sc, sh).astype(BF)
    h_ref[LRU_HALO:LRU_HALO + tm, :] = _normmod(x_ref[0], g, sc, sh).astype(BF)
    h_ref[LRU_HALO + tm:2 * LRU_HALO + tm, :] = _normmod(xn_ref[0], g, sc, sh).astype(BF)
    for j in range(D // PROJ_CH):
        c0 = j * PROJ_CH
        cols = slice(c0, c0 + PROJ_CH)
        gate_ref[0, :, cols] = _gelu_tanh(
            _dot(h_ref[LRU_HALO:LRU_HALO + tm, :], w_ref[:, cols]) + b_ref[:, cols])
        u = _dot(h_ref[...], w_ref[:, D + c0:D + c0 + PROJ_CH]) + b_ref[:, D + c0:D + c0 + PROJ_CH]
        u_ref[0:LRU_HALO, cols] = jnp.where(t > 0, u[0:LRU_HALO], 0.0)
        u_ref[LRU_HALO:LRU_HALO + tm, cols] = u[LRU_HALO:LRU_HALO + tm]
        u_ref[LRU_HALO + tm:, cols] = jnp.where(t < nt - 1, u[LRU_HALO + tm:], 0.0)
    off = LRU_HALO - RNN_CONV_K // 2
    for n in range(D // LANES):
        lanes = slice(n * LANES, (n + 1) * LANES)
        xr = jnp.zeros((tm, LANES), F32)
        for k in range(RNN_CONV_K):
            xr = xr + u_ref[off + k:off + k + tm, lanes] * wc_ref[k:k + 1, lanes]
        xr_ref[0, :, lanes] = xr + bc_ref[:, lanes]


def _lru_in(x, mod, g, w, b, w_conv, b_conv):
    bsz, t, _ = x.shape
    tm = _row_tile(t)
    nt = t // tm
    hb = tm // LRU_HALO
    row = pl.BlockSpec((1, tm, D), lambda bi, ti: (bi, ti, 0))
    return pl.pallas_call(
        functools.partial(_lru_in_kernel, tm=tm, nt=nt),
        grid=(bsz, nt),
        in_specs=[
            pl.BlockSpec((1, LRU_HALO, D), lambda bi, ti: (bi, jnp.maximum(ti * hb - 1, 0), 0)),
            row,
            pl.BlockSpec((1, LRU_HALO, D),
                         lambda bi, ti: (bi, jnp.minimum((ti + 1) * hb, t // LRU_HALO - 1), 0)),
            pl.BlockSpec((1, SUBLANES, D), lambda bi, ti: (bi, 0, 0)),
            _const_spec((1, D)),
            _const_spec((D, 2 * D)),
            _const_spec((1, 2 * D)),
            _const_spec((RNN_CONV_K, D)),
            _const_spec((1, D)),
        ],
        out_specs=[row, row],
        out_shape=[jax.ShapeDtypeStruct((bsz, t, D), F32)] * 2,
        scratch_shapes=[pltpu.VMEM((tm + 2 * LRU_HALO, D), BF),
                        pltpu.VMEM((tm + 2 * LRU_HALO, D), F32)],
        compiler_params=_params(2),
    )(x, x, x, mod, g.reshape(1, D), w, b.reshape(1, 2 * D), w_conv, b_conv.reshape(1, D))


def _lru_scan_kernel(xr_ref, h0_ref, wg_ref, bg_ref, lam_ref, hs_ref, hl_ref,
                     pc_ref, bs_ref, h_ref, *, tm, reverse):
    t = pl.program_id(1)

    @pl.when(t == 0)
    def _():
        h_ref[...] = jnp.broadcast_to(h0_ref[0], (SUBLANES, D))

    nv = tm // SUBLANES
    sub = lax.broadcasted_iota(jnp.int32, (nv, SUBLANES, LANES), 1)

    for n in range(RNN_BLOCKS):
        lanes = slice(n * LANES, (n + 1) * LANES)
        xr = xr_ref[0, :, lanes]
        gts = _dot(xr.astype(BF), wg_ref[n]) + bg_ref[n]
        r = _sigmoid(gts[:, :LANES])
        gi = _sigmoid(gts[:, LANES:])
        nl = -lam_ref[:, lanes]
        sp = jnp.maximum(nl, 0.0) + jnp.log1p(jnp.exp(-jnp.abs(nl)))
        log_a = (-LRU_C) * r * sp
        a = jnp.exp(log_a)
        th = jnp.tanh(log_a)
        one_m_a2 = (-2.0 * th) / (1.0 - th)
        root = jnp.where(one_m_a2 > 0.0, one_m_a2 * lax.rsqrt(one_m_a2), 0.0)
        b = root * (gi * xr)
        a3 = a.reshape(nv, SUBLANES, LANES)
        b3 = b.reshape(nv, SUBLANES, LANES)
        for sh in (1, 2, 4):
            if reverse:
                m = sub < SUBLANES - sh
                a_s = pltpu.roll(a3, SUBLANES - sh, 1)
                b_s = pltpu.roll(b3, SUBLANES - sh, 1)
            else:
                m = sub >= sh
                a_s = pltpu.roll(a3, sh, 1)
                b_s = pltpu.roll(b3, sh, 1)
            b3 = jnp.where(m, a3 * b_s + b3, b3)
            a3 = jnp.where(m, a3 * a_s, a3)
        pc_ref[:, lanes] = a3.reshape(tm, LANES)
        bs_ref[:, lanes] = b3.reshape(tm, LANES)

    edge = 0 if reverse else SUBLANES - 1

    def group(i, hcar):
        v = (nv - 1 - i) if reverse else i
        st = pl.multiple_of(v * SUBLANES, SUBLANES)
        res = bs_ref[pl.ds(st, SUBLANES), :] + pc_ref[pl.ds(st, SUBLANES), :] * hcar
        hs_ref[0, pl.ds(st, SUBLANES), :] = res
        return jnp.broadcast_to(res[edge:edge + 1, :], (SUBLANES, D))

    hfin = lax.fori_loop(0, nv, group, h_ref[...])
    h_ref[...] = hfin
    hl_ref[0] = hfin[0:1, :]


def _lru_scan(xr, h0, wg, bg, lam, reverse):
    bsz, t, _ = xr.shape
    tm = _row_tile(t)
    nt = t // tm
    tidx = (lambda ti: nt - 1 - ti) if reverse else (lambda ti: ti)
    row = pl.BlockSpec((1, tm, D), lambda bi, ti: (bi, tidx(ti), 0))
    hs, hl = pl.pallas_call(
        functools.partial(_lru_scan_kernel, tm=tm, reverse=reverse),
        grid=(bsz, nt),
        in_specs=[
            row,
            pl.BlockSpec((1, 1, D), lambda bi, ti: (bi, 0, 0)),
            _const_spec((RNN_BLOCKS, RNN_BLOCK, 2 * RNN_BLOCK)),
            _const_spec((RNN_BLOCKS, 1, 2 * RNN_BLOCK)),
            _const_spec((1, D)),
        ],
        out_specs=[row, pl.BlockSpec((1, 1, D), lambda bi, ti: (bi, 0, 0))],
        out_shape=[jax.ShapeDtypeStruct((bsz, t, D), F32), jax.ShapeDtypeStruct((bsz, 1, D), F32)],
        scratch_shapes=[pltpu.VMEM((tm, D), F32), pltpu.VMEM((tm, D), F32),
                        pltpu.VMEM((SUBLANES, D), F32)],
        compiler_params=_params(2),
    )(xr, h0, wg, bg, lam.reshape(1, D))
    return hs, hl


NA_QR = 4
NA_SR = 12
NA_HG = 4
NA_NQ = NA_QR * GRID_W
NA_QC = NA_NQ
NA_NK = NA_SR * GRID_W
NA_ROWS = 8192 // GRID_W
NA_PAD = NA_KH // 2
NA_CASES = (0, NA_QR, NA_ROWS - NA_QR)


def _na_case_geometry(r):
    rs = min(max(r - NA_KH // 2, 0), NA_ROWS - NA_SR)
    qr = r + np.arange(NA_QR)
    r0 = np.clip(qr - NA_KH // 2, 0, NA_ROWS - NA_KH)
    kr = rs + np.arange(NA_SR)
    row_ok = (kr[None, :] >= r0[:, None]) & (kr[None, :] < r0[:, None] + NA_KH)
    a0 = [NA_PAD + rs - int(q) + NA_KH - 1 for q in qr]
    return row_ok, a0


def _na_table_kernel(cm2_ref, o_ref):
    lo = lax.broadcasted_iota(jnp.int32, (GRID_W, LANES), 1) < GRID_W
    for ci, r in enumerate(NA_CASES):
        row_ok, a0 = _na_case_geometry(r)
        for qi in range(NA_QR):
            for j in range(NA_SR // 2):
                blk = cm2_ref[0, a0[qi] + 2 * j]
                ok_l, ok_r = bool(row_ok[qi, 2 * j]), bool(row_ok[qi, 2 * j + 1])
                if not (ok_l or ok_r):
                    blk = jnp.full((GRID_W, LANES), NEG, F32)
                elif not ok_r:
                    blk = jnp.where(lo, blk, NEG)
                elif not ok_l:
                    blk = jnp.where(lo, NEG, blk)
                o_ref[ci, 0, qi * GRID_W:(qi + 1) * GRID_W, j * LANES:(j + 1) * LANES] = blk


def _na_bias_tables(rpb):
    cols = np.arange(GRID_W)
    c0 = np.clip(cols - NA_KW // 2, 0, GRID_W - NA_KW)
    col_ok = (cols[None, :] >= c0[:, None]) & (cols[None, :] < c0[:, None] + NA_KW)
    col_off = cols[None, :] - cols[:, None] + (NA_KW - 1)
    onehot = ((col_off[..., None] == np.arange(2 * NA_KW - 1)) & col_ok[..., None]).astype(np.float32)
    cm = jnp.einsum("hab,ckb->hack", rpb, jnp.asarray(onehot), precision=lax.Precision.HIGHEST)
    cm = jnp.where(jnp.asarray(col_ok), cm, NEG)
    cm = jnp.pad(cm, ((0, 0), (NA_PAD, NA_PAD), (0, 0), (0, 0)), constant_values=NEG)
    cm2 = jnp.concatenate([cm[:, :-1], cm[:, 1:]], axis=-1)
    na = cm2.shape[1]
    return pl.pallas_call(
        _na_table_kernel,
        grid=(HEADS,),
        in_specs=[pl.BlockSpec((1, na, GRID_W, LANES), lambda h: (h, 0, 0, 0))],
        out_specs=pl.BlockSpec((len(NA_CASES), 1, NA_NQ, NA_NK), lambda h: (0, h, 0, 0)),
        out_shape=jax.ShapeDtypeStruct((len(NA_CASES), HEADS, NA_NQ, NA_NK), F32),
        compiler_params=_params(1),
    )(cm2)


def _na_kernel(q_ref, k_ref, v_ref, kc_ref, vc_ref, bias_ref, o_ref):
    ri = pl.program_id(2)
    rs = jnp.clip(ri * NA_QR - NA_KH // 2, 0, NA_ROWS - NA_SR)
    st = pl.multiple_of(rs * GRID_W, GRID_W)
    lo = lax.broadcasted_iota(jnp.int32, (1, LANES), 1) < HD
    for hp in range(NA_HG // 2):
        lanes = slice(hp * LANES, (hp + 1) * LANES)
        kp = k_ref[0, pl.ds(st, NA_NK), lanes]
        kcp = kc_ref[0, :, lanes]
        vp = jnp.concatenate([v_ref[0, pl.ds(st, NA_NK), lanes], vc_ref[0, :, lanes]], axis=0)
        for qc in range(NA_NQ // NA_QC):
            rows = slice(qc * NA_QC, (qc + 1) * NA_QC)
            qp = q_ref[0, rows, lanes]
            res = []
            for hh in range(2):
                own = lo if hh == 0 else jnp.logical_not(lo)
                qh = jnp.where(own, qp, jnp.zeros_like(qp))
                s = jnp.concatenate([_dot_nt(qh, kp) + bias_ref[0, 2 * hp + hh, rows, :],
                                     _dot_nt(qh, kcp)], axis=-1)
                p = jnp.exp(s - jnp.max(s, axis=-1, keepdims=True)).astype(BF)
                res.append(_dot(p, jnp.where(own, vp, jnp.ones_like(vp))))
            num = jnp.where(lo, res[0], res[1])
            den = pltpu.roll(jnp.where(lo, res[1], res[0]), HD, 1)
            o_ref[0, rows, lanes] = (num / den).astype(BF)


def _na_attention(qkv, qkv_c, bias_tabs):
    bsz, s, _ = qkv.shape
    nr = NA_ROWS // NA_QR
    hw = NA_HG * HD
    nhg = HEADS // NA_HG

    def case(ri):
        return jnp.where(ri == 0, 0, jnp.where(ri == nr - 1, 2, 1))

    return pl.pallas_call(
        _na_kernel,
        grid=(bsz, nhg, nr),
        in_specs=[
            pl.BlockSpec((1, NA_NQ, hw), lambda bi, hg, ri: (bi, ri, hg)),
            pl.BlockSpec((1, s, hw), lambda bi, hg, ri: (bi, 0, nhg + hg)),
            pl.BlockSpec((1, s, hw), lambda bi, hg, ri: (bi, 0, 2 * nhg + hg)),
            pl.BlockSpec((1, CTX, hw), lambda bi, hg, ri: (bi, 0, nhg + hg)),
            pl.BlockSpec((1, CTX, hw), lambda bi, hg, ri: (bi, 0, 2 * nhg + hg)),
            pl.BlockSpec((1, NA_HG, NA_NQ, NA_NK), lambda bi, hg, ri: (case(ri), hg, 0, 0)),
        ],
        out_specs=pl.BlockSpec((1, NA_NQ, hw), lambda bi, hg, ri: (bi, ri, hg)),
        out_shape=jax.ShapeDtypeStruct((bsz, s, D), BF),
        compiler_params=_params(3),
    )(qkv, qkv, qkv, qkv_c, qkv_c, bias_tabs)


def kernel(x, c, ctx, c_ctx, w_mod, b_mod, norm_g, w_ffn_in, w_ffn_out, a_w_pw1, a_b_pw1, a_w_dw, a_b_dw, a_ln_g, a_ln_b, a_w_pw2, a_b_pw2, b_w_in, b_b_in, b_w_conv, b_b_conv, b_w_rg, b_b_rg, b_w_ig, b_b_ig, b_lam, b_w_out, b_b_out, c_w_qkv, c_b_qkv, c_rpb, c_w_o, c_b_o, final_g):
    bsz = x.shape[0]
    cond = jnp.concatenate([c, c_ctx[None], jnp.zeros((SUBLANES - bsz - 1, D), F32)], axis=0)
    mods = _mod_call(cond, w_mod, b_mod)

    def layer_mods(i):
        m = mods[i].reshape(SUBLANES, 6, D)
        pad = ((0, 0), (0, SUBLANES - 6), (0, 0))
        lat = jnp.pad(m[:bsz], pad)
        cx = jnp.pad(jnp.broadcast_to(m[bsz:bsz + 1], (bsz, 6, D)), pad)
        return lat, cx

    def conv_layer(xs, mod, i, j):
        u = _in_proj(xs, mod, norm_g[i, 0], a_w_pw1[j].astype(BF), a_b_pw1[j], "glu")
        return _conv_block(xs, u, mod, a_w_dw[j], a_b_dw[j], a_ln_g[j], a_ln_b[j],
                           a_w_pw2[j].astype(BF), a_b_pw2[j])

    w_ffn_in_bf = w_ffn_in.astype(BF)
    w_ffn_out_bf = w_ffn_out.astype(BF)

    def ffn(xs, mod, i, final_g=None, **proj):
        return _ffn(xs, mod, norm_g[i, 1], w_ffn_in_bf, w_ffn_out_bf, i, final_g, **proj)

    cs = ctx

    lat, cx = layer_mods(0)
    x = ffn(conv_layer(x, lat, 0, 0), lat, 0)
    cs = ffn(conv_layer(cs, cx, 0, 0), cx, 0)

    lat, cx = layer_mods(1)
    w_in = b_w_in[0].astype(BF)
    w_out = b_w_out[0].astype(BF)
    gate_l, xr_l = _lru_in(x, lat, norm_g[1, 0], w_in, b_b_in[0], b_w_conv[0], b_b_conv[0])
    gate_c, xr_c = _lru_in(cs, cx, norm_g[1, 0], w_in, b_b_in[0], b_w_conv[0], b_b_conv[0])
    hs_l, hs_c = [], []
    for d, reverse in ((0, False), (1, True)):
        wg = jnp.concatenate([b_w_rg[0, d], b_w_ig[0, d]], axis=-1).astype(BF)
        bg = jnp.concatenate([b_b_rg[0, d].reshape(RNN_BLOCKS, 1, RNN_BLOCK),
                              b_b_ig[0, d].reshape(RNN_BLOCKS, 1, RNN_BLOCK)], axis=-1)
        h0 = jnp.zeros((bsz, 1, D), F32)
        hc, hc_last = _lru_scan(xr_c, h0, wg, bg, b_lam[0, d], reverse)
        hl, _ = _lru_scan(xr_l, hc_last, wg, bg, b_lam[0, d], reverse)
        hs_c.append(hc)
        hs_l.append(hl)
    x = ffn(x, lat, 1, ys=(hs_l[0], hs_l[1], gate_l), w_p=w_out, b_p=b_b_out[0])
    cs = ffn(cs, cx, 1, ys=(hs_c[0], hs_c[1], gate_c), w_p=w_out, b_p=b_b_out[0])

    lat, cx = layer_mods(2)
    w_qkv = c_w_qkv[0].astype(BF)
    qkv = _in_proj(x, lat, norm_g[2, 0], w_qkv, c_b_qkv[0], "qkv")
    qkv_c = _in_proj(cs, cx, norm_g[2, 0], w_qkv, c_b_qkv[0], "qkv")
    o = _na_attention(qkv, qkv_c, _na_bias_tables(c_rpb[0]))
    x = ffn(x, lat, 2, ys=(o,), w_p=c_w_o[0].astype(BF), b_p=c_b_o[0])

    lat, _ = layer_mods(3)
    return ffn(conv_layer(x, lat, 3, 1), lat, 3, final_g)
```

```python
import functools

import jax
import jax.numpy as jnp
import numpy as np
from jax import lax
from jax.experimental import pallas as pl
from jax.experimental.pallas import tpu as pltpu

D = 1024
DEPTH = 4
GRID_W = 64
CTX = 256
CONV_K = 31
RNN_BLOCKS = 8
RNN_BLOCK = 128
RNN_CONV_K = 4
LRU_C = 8.0
HEADS = 16
HD = 64
NA_KH = 8
NA_KW = 16
DFF = 2816
EPS = 1e-6

LANES = 128
SUBLANES = 8
VMEM_LIMIT = 56 * 1024 * 1024
NEG = -1e30

BF = jnp.bfloat16
F32 = jnp.float32


def _dot(a, b):
    return jnp.dot(a, b, preferred_element_type=F32)


def _dot_nt(a, b):
    return lax.dot_general(a, b, (((1,), (1,)), ((), ())), preferred_element_type=F32)


def _params(n_axes):
    return pltpu.CompilerParams(
        dimension_semantics=("arbitrary",) * n_axes, vmem_limit_bytes=VMEM_LIMIT)


def _const_spec(shape):
    nd = len(shape)
    return pl.BlockSpec(shape, lambda *_: (0,) * nd, pipeline_mode=pl.Buffered(1))


def _row_tile(t):
    return min(512, t)


def _normmod(x, g, sc, sh):
    ms = jnp.mean(x * x, axis=-1, keepdims=True)
    return (x * lax.rsqrt(ms + EPS) * g) * (1.0 + sc) + sh


def _sigmoid(x):
    return 0.5 * jnp.tanh(0.5 * x) + 0.5


def _silu(x):
    h = 0.5 * x
    return h * jnp.tanh(h) + h


def _gelu_tanh(x):
    c = 0.7978845608028654
    th = jnp.tanh(x * ((x * x) * (c * 0.044715) + c))
    hx = 0.5 * x
    return hx + hx * th


MOD_TN = 1536


def _mod_kernel(c_ref, w_ref, b_ref, o_ref):
    cnd = c_ref[...]
    s = (cnd * _sigmoid(cnd)).astype(BF)
    o_ref[0] = _dot(s, w_ref[0].astype(BF)) + b_ref[0]


def _mod_call(cond, w_mod, b_mod):
    n = 6 * D
    return pl.pallas_call(
        _mod_kernel,
        grid=(DEPTH, n // MOD_TN),
        in_specs=[
            pl.BlockSpec((SUBLANES, D), lambda i, j: (0, 0)),
            pl.BlockSpec((1, D, MOD_TN), lambda i, j: (i, 0, j)),
            pl.BlockSpec((1, 1, MOD_TN), lambda i, j: (i, 0, j)),
        ],
        out_specs=pl.BlockSpec((1, SUBLANES, MOD_TN), lambda i, j: (i, 0, j)),
        out_shape=jax.ShapeDtypeStruct((DEPTH, SUBLANES, n), F32),
        compiler_params=_params(2),
    )(cond, w_mod, b_mod.reshape(DEPTH, 1, n))


PROJ_CH = 512


def _in_proj_kernel(x_ref, mod_ref, g_ref, w_ref, b_ref, o_ref, *, variant):
    mod = mod_ref[0]
    h = _normmod(x_ref[0], g_ref[...], mod[1:2], mod[0:1]).astype(BF)

    def proj(c0):
        return _dot(h, w_ref[:, c0:c0 + PROJ_CH]) + b_ref[:, c0:c0 + PROJ_CH]

    for j in range(D // PROJ_CH):
        c0 = j * PROJ_CH
        if variant == "glu":
            o_ref[0, :, c0:c0 + PROJ_CH] = proj(c0) * _sigmoid(proj(D + c0))
        else:
            o_ref[0, :, c0:c0 + PROJ_CH] = (proj(c0) * (HD ** -0.5)).astype(BF)
            o_ref[0, :, D + c0:D + c0 + PROJ_CH] = proj(D + c0).astype(BF)
            o_ref[0, :, 2 * D + c0:2 * D + c0 + PROJ_CH] = proj(2 * D + c0).astype(BF)


def _in_proj(x, mod, g, w, b, variant):
    bsz, t, _ = x.shape
    tm = _row_tile(t)
    n = w.shape[1]
    row = lambda width: pl.BlockSpec((1, tm, width), lambda bi, ti: (bi, ti, 0))
    if variant == "glu":
        out_shape, out_spec = jax.ShapeDtypeStruct((bsz, t, D), F32), row(D)
    else:
        out_shape, out_spec = jax.ShapeDtypeStruct((bsz, t, 3 * D), BF), row(3 * D)
    return pl.pallas_call(
        functools.partial(_in_proj_kernel, variant=variant),
        grid=(bsz, t // tm),
        in_specs=[
            row(D),
            pl.BlockSpec((1, SUBLANES, D), lambda bi, ti: (bi, 0, 0)),
            _const_spec((1, D)),
            _const_spec((D, n)),
            _const_spec((1, n)),
        ],
        out_specs=out_spec,
        out_shape=out_shape,
        compiler_params=_params(2),
    )(x, mod, g.reshape(1, D), w, b.reshape(1, n))


CONV_HALO = 16
CONV_RC = 64
CONV_TM = 256
CONV_SPAN = (CONV_K - 1) // SUBLANES * SUBLANES


def _conv_kernel(x_ref, up_ref, uc_ref, un_ref, mod_ref, wdw_ref, bdw_ref, lng_ref, lnb_ref,
                 w2_ref, b2_ref, o_ref, sh_ref, acc_ref, *, tm, nt):
    t = pl.program_id(1)
    for cb in range(D // LANES):
        lanes = slice(cb * LANES, (cb + 1) * LANES)
        sh_ref[0, cb, 0:CONV_HALO, :] = jnp.where(t > 0, up_ref[0, :, lanes], 0.0)
        sh_ref[0, cb, CONV_HALO:CONV_HALO + tm, :] = uc_ref[0, :, lanes]
        sh_ref[0, cb, CONV_HALO + tm:2 * CONV_HALO + tm, :] = jnp.where(
            t < nt - 1, un_ref[0, :, lanes], 0.0)
    span = tm + CONV_SPAN
    for s in range(1, SUBLANES):
        for cb in range(D // LANES):
            sh_ref[s, cb, 0:span, :] = sh_ref[0, cb, s:s + span, :]
    off = CONV_HALO - CONV_K // 2

    def rows(rc, carry):
        base = pl.multiple_of(rc * CONV_RC, CONV_RC)
        for cb in range(D // LANES):
            lanes = slice(cb * LANES, (cb + 1) * LANES)
            acc = jnp.zeros((CONV_RC, LANES), F32)
            for k in range(CONV_K):
                s = (off + k) % SUBLANES
                q = off + k - s
                acc = acc + sh_ref[s, cb, pl.ds(base + q, CONV_RC), :] * wdw_ref[k:k + 1, lanes]
            acc_ref[pl.ds(base, CONV_RC), lanes] = acc + bdw_ref[:, lanes]
        return carry

    lax.fori_loop(0, tm // CONV_RC, rows, 0)

    v = acc_ref[...]
    mu = jnp.mean(v, axis=-1, keepdims=True)
    vc = v - mu
    var = jnp.mean(vc * vc, axis=-1, keepdims=True)
    y = vc * lax.rsqrt(var + EPS) * lng_ref[...] + lnb_ref[...]
    s_act = _silu(y).astype(BF)
    y2 = _dot(s_act, w2_ref[...]) + b2_ref[...]
    o_ref[0] = x_ref[0] + mod_ref[0][2:3] * y2


def _conv_block(x, u, mod, w_dw, b_dw, ln_g, ln_b, w2, b2):
    bsz, t, _ = x.shape
    tm = CONV_TM
    nt = t // tm
    hb = tm // CONV_HALO
    row = pl.BlockSpec((1, tm, D), lambda bi, ti: (bi, ti, 0))
    vec = lambda a: a.reshape(1, D)
    return pl.pallas_call(
        functools.partial(_conv_kernel, tm=tm, nt=nt),
        grid=(bsz, nt),
        in_specs=[
            row,
            pl.BlockSpec((1, CONV_HALO, D), lambda bi, ti: (bi, jnp.maximum(ti * hb - 1, 0), 0)),
            row,
            pl.BlockSpec((1, CONV_HALO, D),
                         lambda bi, ti: (bi, jnp.minimum((ti + 1) * hb, t // CONV_HALO - 1), 0)),
            pl.BlockSpec((1, SUBLANES, D), lambda bi, ti: (bi, 0, 0)),
            _const_spec((CONV_K, D)),
            _const_spec((1, D)),
            _const_spec((1, D)),
            _const_spec((1, D)),
            _const_spec((D, D)),
            _const_spec((1, D)),
        ],
        out_specs=row,
        out_shape=jax.ShapeDtypeStruct((bsz, t, D), F32),
        scratch_shapes=[pltpu.VMEM((SUBLANES, D // LANES, tm + 2 * CONV_HALO, LANES), F32),
                        pltpu.VMEM((tm, D), F32)],
        compiler_params=_params(2),
    )(x, u, u, u, mod, w_dw, vec(b_dw), vec(ln_g), vec(ln_b), w2, vec(b2))


FFN_CH = 256


def _ffn_body(x, mod, g_ref, win_ref, wout_ref, fg_ref, acc_ref, final):
    h = _normmod(x, g_ref[...], mod[4:5], mod[3:4]).astype(BF)
    for j in range(DFF // FFN_CH):
        c0 = j * FFN_CH
        u1 = _dot(h, win_ref[:, c0:c0 + FFN_CH])
        u2 = _dot(h, win_ref[:, DFF + c0:DFF + c0 + FFN_CH])
        a = (_silu(u1) * u2).astype(BF)
        part = _dot(a, wout_ref[c0:c0 + FFN_CH, :])
        if j == 0:
            acc_ref[...] = part
        else:
            acc_ref[...] += part
    out = x + mod[5:6] * acc_ref[...]
    if final:
        ms = jnp.mean(out * out, axis=-1, keepdims=True)
        out = out * lax.rsqrt(ms + EPS) * fg_ref[...]
    return out


def _ffn_kernel(x_ref, mod_ref, *refs, final, n_y):
    y_refs = refs[:n_y]
    if n_y:
        wp_ref, bp_ref = refs[n_y:n_y + 2]
        refs = refs[n_y + 2:]
    g_ref, win_ref, wout_ref, fg_ref, o_ref, acc_ref = refs[:6]
    mod = mod_ref[0]
    if n_y == 0:
        x = x_ref[0]
    else:
        x1_ref = refs[6]
        if n_y == 3:
            y = ((y_refs[0][0] + y_refs[1][0]) * y_refs[2][0]).astype(BF)
        else:
            y = y_refs[0][0]
        x1_ref[...] = x_ref[0] + mod[2:3] * (_dot(y, wp_ref[...]) + bp_ref[...])
        x = x1_ref[...]
    o_ref[0] = _ffn_body(x, mod, g_ref, win_ref, wout_ref, fg_ref, acc_ref, final)


def _ffn(x, mod, g, w_in, w_out, layer, final_g=None, ys=(), w_p=None, b_p=None):
    bsz, t, _ = x.shape
    tm = min(1024, t) if not ys else _row_tile(t)
    final = final_g is not None
    fg = (final_g if final else jnp.ones((D,), F32)).reshape(1, D)
    row = pl.BlockSpec((1, tm, D), lambda bi, ti: (bi, ti, 0))
    layer_spec = lambda r, c: pl.BlockSpec((None, r, c), lambda *_: (layer, 0, 0),
                                           pipeline_mode=pl.Buffered(1))
    n_y = len(ys)
    proj_specs = [row] * n_y + ([_const_spec((D, D)), _const_spec((1, D))] if n_y else [])
    proj_args = list(ys) + ([w_p, b_p.reshape(1, D)] if n_y else [])
    scratch = [pltpu.VMEM((tm, D), F32)] * (2 if n_y else 1)
    return pl.pallas_call(
        functools.partial(_ffn_kernel, final=final, n_y=n_y),
        grid=(bsz, t // tm),
        in_specs=[row, pl.BlockSpec((1, SUBLANES, D), lambda bi, ti: (bi, 0, 0))] + proj_specs + [
            _const_spec((1, D)),
            layer_spec(D, 2 * DFF),
            layer_spec(DFF, D),
            _const_spec((1, D)),
        ],
        out_specs=row,
        out_shape=jax.ShapeDtypeStruct((bsz, t, D), F32),
        scratch_shapes=scratch,
        compiler_params=_params(2),
    )(x, mod, *proj_args, g.reshape(1, D), w_in, w_out, fg)


LRU_HALO = 16


def _lru_in_kernel(xp_ref, x_ref, xn_ref, mod_ref, g_ref, w_ref, b_ref, wc_ref, bc_ref,
                   gate_ref, xr_ref, h_ref, u_ref, *, tm, nt):
    t = pl.program_id(1)
    mod = mod_ref[0]
    g, sc, sh = g_ref[...], mod[1:2], mod[0:1]
    h_ref[0:LRU_HALO, :] = _normmod(xp_ref[0], g, sc, sh).astype(BF)
    h_ref[LRU_HALO:LRU_HALO + tm, :] = _normmod(x_ref[0], g, sc, sh).astype(BF)
    h_ref[LRU_HALO + tm:2 * LRU_HALO + tm, :] = _normmod(xn_ref[0], g, sc, sh).astype(BF)
    for j in range(D // PROJ_CH):
        c0 = j * PROJ_CH
        cols = slice(c0, c0 + PROJ_CH)
        gate_ref[0, :, cols] = _gelu_tanh(
            _dot(h_ref[LRU_HALO:LRU_HALO + tm, :], w_ref[:, cols]) + b_ref[:, cols])
        u = _dot(h_ref[...], w_ref[:, D + c0:D + c0 + PROJ_CH]) + b_ref[:, D + c0:D + c0 + PROJ_CH]
        u_ref[0:LRU_HALO, cols] = jnp.where(t > 0, u[0:LRU_HALO], 0.0)
        u_ref[LRU_HALO:LRU_HALO + tm, cols] = u[LRU_HALO:LRU_HALO + tm]
        u_ref[LRU_HALO + tm:, cols] = jnp.where(t < nt - 1, u[LRU_HALO + tm:], 0.0)
    off = LRU_HALO - RNN_CONV_K // 2
    for n in range(D // LANES):
        lanes = slice(n * LANES, (n + 1) * LANES)
        xr = jnp.zeros((tm, LANES), F32)
        for k in range(RNN_CONV_K):
            xr = xr + u_ref[off + k:off + k + tm, lanes] * wc_ref[k:k + 1, lanes]
        xr_ref[0, :, lanes] = xr + bc_ref[:, lanes]


def _lru_in(x, mod, g, w, b, w_conv, b_conv):
    bsz, t, _ = x.shape
    tm = _row_tile(t)
    nt = t // tm
    hb = tm // LRU_HALO
    row = pl.BlockSpec((1, tm, D), lambda bi, ti: (bi, ti, 0))
    return pl.pallas_call(
        functools.partial(_lru_in_kernel, tm=tm, nt=nt),
        grid=(bsz, nt),
        in_specs=[
            pl.BlockSpec((1, LRU_HALO, D), lambda bi, ti: (bi, jnp.maximum(ti * hb - 1, 0), 0)),
            row,
            pl.BlockSpec((1, LRU_HALO, D),
                         lambda bi, ti: (bi, jnp.minimum((ti + 1) * hb, t // LRU_HALO - 1), 0)),
            pl.BlockSpec((1, SUBLANES, D), lambda bi, ti: (bi, 0, 0)),
            _const_spec((1, D)),
            _const_spec((D, 2 * D)),
            _const_spec((1, 2 * D)),
            _const_spec((RNN_CONV_K, D)),
            _const_spec((1, D)),
        ],
        out_specs=[row, row],
        out_shape=[jax.ShapeDtypeStruct((bsz, t, D), F32)] * 2,
        scratch_shapes=[pltpu.VMEM((tm + 2 * LRU_HALO, D), BF),
                        pltpu.VMEM((tm + 2 * LRU_HALO, D), F32)],
        compiler_params=_params(2),
    )(x, x, x, mod, g.reshape(1, D), w, b.reshape(1, 2 * D), w_conv, b_conv.reshape(1, D))


def _lru_scan_kernel(xr_ref, h0_ref, wg_ref, bg_ref, lam_ref, hs_ref, hl_ref,
                     pc_ref, bs_ref, h_ref, *, tm, reverse):
    t = pl.program_id(1)

    @pl.when(t == 0)
    def _():
        h_ref[...] = jnp.broadcast_to(h0_ref[0], (SUBLANES, D))

    nv = tm // SUBLANES
    sub = lax.broadcasted_iota(jnp.int32, (nv, SUBLANES, LANES), 1)

    for n in range(RNN_BLOCKS):
        lanes = slice(n * LANES, (n + 1) * LANES)
        xr = xr_ref[0, :, lanes]
        gh = _dot(xr.astype(BF), wg_ref[n]) + bg_ref[n]
        tr = jnp.tanh(gh[:, :LANES])
        tg = jnp.tanh(gh[:, LANES:])
        nl = -lam_ref[:, lanes]
        sp = jnp.maximum(nl, 0.0) + jnp.log1p(jnp.exp(-jnp.abs(nl)))
        c1 = (-0.5 * LRU_C) * sp
        log_a = c1 * tr + c1
        a = jnp.exp(log_a)
        th = jnp.tanh(log_a)
        one_m_a2 = (-2.0 * th) / (1.0 - th)
        root = jnp.where(one_m_a2 > 0.0, one_m_a2 * lax.rsqrt(one_m_a2), 0.0)
        hx = 0.5 * xr
        b = root * (hx * tg + hx)
        a3 = a.reshape(nv, SUBLANES, LANES)
        b3 = b.reshape(nv, SUBLANES, LANES)
        for sh in (1, 2, 4):
            if reverse:
                m = sub < SUBLANES - sh
                a_s = pltpu.roll(a3, SUBLANES - sh, 1)
                b_s = pltpu.roll(b3, SUBLANES - sh, 1)
            else:
                m = sub >= sh
                a_s = pltpu.roll(a3, sh, 1)
                b_s = pltpu.roll(b3, sh, 1)
            b3 = jnp.where(m, a3 * b_s + b3, b3)
            a3 = jnp.where(m, a3 * a_s, a3)
        pc_ref[:, lanes] = a3.reshape(tm, LANES)
        bs_ref[:, lanes] = b3.reshape(tm, LANES)

    edge = 0 if reverse else SUBLANES - 1

    def group(i, hcar):
        v = (nv - 1 - i) if reverse else i
        st = pl.multiple_of(v * SUBLANES, SUBLANES)
        res = bs_ref[pl.ds(st, SUBLANES), :] + pc_ref[pl.ds(st, SUBLANES), :] * hcar
        hs_ref[0, pl.ds(st, SUBLANES), :] = res
        return jnp.broadcast_to(res[edge:edge + 1, :], (SUBLANES, D))

    hfin = lax.fori_loop(0, nv, group, h_ref[...])
    h_ref[...] = hfin
    hl_ref[0] = hfin[0:1, :]


def _lru_scan(xr, h0, wg, bg, lam, reverse):
    bsz, t, _ = xr.shape
    tm = _row_tile(t)
    nt = t // tm
    tidx = (lambda ti: nt - 1 - ti) if reverse else (lambda ti: ti)
    row = pl.BlockSpec((1, tm, D), lambda bi, ti: (bi, tidx(ti), 0))
    hs, hl = pl.pallas_call(
        functools.partial(_lru_scan_kernel, tm=tm, reverse=reverse),
        grid=(bsz, nt),
        in_specs=[
            row,
            pl.BlockSpec((1, 1, D), lambda bi, ti: (bi, 0, 0)),
            _const_spec((RNN_BLOCKS, RNN_BLOCK, 2 * RNN_BLOCK)),
            _const_spec((RNN_BLOCKS, 1, 2 * RNN_BLOCK)),
            _const_spec((1, D)),
        ],
        out_specs=[row, pl.BlockSpec((1, 1, D), lambda bi, ti: (bi, 0, 0))],
        out_shape=[jax.ShapeDtypeStruct((bsz, t, D), F32), jax.ShapeDtypeStruct((bsz, 1, D), F32)],
        scratch_shapes=[pltpu.VMEM((tm, D), F32), pltpu.VMEM((tm, D), F32),
                        pltpu.VMEM((SUBLANES, D), F32)],
        compiler_params=_params(2),
    )(xr, h0, wg, bg, lam.reshape(1, D))
    return hs, hl


NA_QR = 4
NA_SR = 12
NA_HG = 4
NA_NQ = NA_QR * GRID_W
NA_QC = NA_NQ
NA_NK = NA_SR * GRID_W
NA_ROWS = 8192 // GRID_W
NA_PAD = NA_KH // 2
NA_CASES = (0, NA_QR, NA_ROWS - NA_QR)


def _na_case_geometry(r):
    rs = min(max(r - NA_KH // 2, 0), NA_ROWS - NA_SR)
    qr = r + np.arange(NA_QR)
    r0 = np.clip(qr - NA_KH // 2, 0, NA_ROWS - NA_KH)
    kr = rs + np.arange(NA_SR)
    row_ok = (kr[None, :] >= r0[:, None]) & (kr[None, :] < r0[:, None] + NA_KH)
    a0 = [NA_PAD + rs - int(q) + NA_KH - 1 for q in qr]
    return row_ok, a0


def _na_table_kernel(cm2_ref, o_ref):
    lo = lax.broadcasted_iota(jnp.int32, (GRID_W, LANES), 1) < GRID_W
    for ci, r in enumerate(NA_CASES):
        row_ok, a0 = _na_case_geometry(r)
        for qi in range(NA_QR):
            for j in range(NA_SR // 2):
                blk = cm2_ref[0, a0[qi] + 2 * j]
                ok_l, ok_r = bool(row_ok[qi, 2 * j]), bool(row_ok[qi, 2 * j + 1])
                if not (ok_l or ok_r):
                    blk = jnp.full((GRID_W, LANES), NEG, F32)
                elif not ok_r:
                    blk = jnp.where(lo, blk, NEG)
                elif not ok_l:
                    blk = jnp.where(lo, NEG, blk)
                o_ref[ci, 0, qi * GRID_W:(qi + 1) * GRID_W, j * LANES:(j + 1) * LANES] = blk


def _na_bias_tables(rpb):
    cols = np.arange(GRID_W)
    c0 = np.clip(cols - NA_KW // 2, 0, GRID_W - NA_KW)
    col_ok = (cols[None, :] >= c0[:, None]) & (cols[None, :] < c0[:, None] + NA_KW)
    col_off = cols[None, :] - cols[:, None] + (NA_KW - 1)
    onehot = ((col_off[..., None] == np.arange(2 * NA_KW - 1)) & col_ok[..., None]).astype(np.float32)
    cm = jnp.einsum("hab,ckb->hack", rpb, jnp.asarray(onehot), precision=lax.Precision.HIGHEST)
    cm = jnp.where(jnp.asarray(col_ok), cm, NEG)
    cm = jnp.pad(cm, ((0, 0), (NA_PAD, NA_PAD), (0, 0), (0, 0)), constant_values=NEG)
    cm2 = jnp.concatenate([cm[:, :-1], cm[:, 1:]], axis=-1)
    na = cm2.shape[1]
    return pl.pallas_call(
        _na_table_kernel,
        grid=(HEADS,),
        in_specs=[pl.BlockSpec((1, na, GRID_W, LANES), lambda h: (h, 0, 0, 0))],
        out_specs=pl.BlockSpec((len(NA_CASES), 1, NA_NQ, NA_NK), lambda h: (0, h, 0, 0)),
        out_shape=jax.ShapeDtypeStruct((len(NA_CASES), HEADS, NA_NQ, NA_NK), F32),
        compiler_params=_params(1),
    )(cm2)


def _na_kernel(q_ref, k_ref, v_ref, kc_ref, vc_ref, bias_ref, o_ref):
    ri = pl.program_id(2)
    rs = jnp.clip(ri * NA_QR - NA_KH // 2, 0, NA_ROWS - NA_SR)
    st = pl.multiple_of(rs * GRID_W, GRID_W)
    lo = lax.broadcasted_iota(jnp.int32, (1, LANES), 1) < HD
    for hp in range(NA_HG // 2):
        lanes = slice(hp * LANES, (hp + 1) * LANES)
        kp = k_ref[0, pl.ds(st, NA_NK), lanes]
        kcp = kc_ref[0, :, lanes]
        vp = jnp.concatenate([v_ref[0, pl.ds(st, NA_NK), lanes], vc_ref[0, :, lanes]], axis=0)
        for qc in range(NA_NQ // NA_QC):
            rows = slice(qc * NA_QC, (qc + 1) * NA_QC)
            qp = q_ref[0, rows, lanes]
            res = []
            for hh in range(2):
                own = lo if hh == 0 else jnp.logical_not(lo)
                qh = jnp.where(own, qp, jnp.zeros_like(qp))
                s = jnp.concatenate([_dot_nt(qh, kp) + bias_ref[0, 2 * hp + hh, rows, :],
                                     _dot_nt(qh, kcp)], axis=-1)
                p = jnp.exp(s - jnp.max(s, axis=-1, keepdims=True)).astype(BF)
                res.append(_dot(p, jnp.where(own, vp, jnp.ones_like(vp))))
            num = jnp.where(lo, res[0], res[1])
            den = pltpu.roll(jnp.where(lo, res[1], res[0]), HD, 1)
            o_ref[0, rows, lanes] = (num / den).astype(BF)


def _na_attention(qkv, qkv_c, bias_tabs):
    bsz, s, _ = qkv.shape
    nr = NA_ROWS // NA_QR
    hw = NA_HG * HD
    nhg = HEADS // NA_HG

    def case(ri):
        return jnp.where(ri == 0, 0, jnp.where(ri == nr - 1, 2, 1))

    return pl.pallas_call(
        _na_kernel,
        grid=(bsz, nhg, nr),
        in_specs=[
            pl.BlockSpec((1, NA_NQ, hw), lambda bi, hg, ri: (bi, ri, hg)),
            pl.BlockSpec((1, s, hw), lambda bi, hg, ri: (bi, 0, nhg + hg)),
            pl.BlockSpec((1, s, hw), lambda bi, hg, ri: (bi, 0, 2 * nhg + hg)),
            pl.BlockSpec((1, CTX, hw), lambda bi, hg, ri: (bi, 0, nhg + hg)),
            pl.BlockSpec((1, CTX, hw), lambda bi, hg, ri: (bi, 0, 2 * nhg + hg)),
            pl.BlockSpec((1, NA_HG, NA_NQ, NA_NK), lambda bi, hg, ri: (case(ri), hg, 0, 0)),
        ],
        out_specs=pl.BlockSpec((1, NA_NQ, hw), lambda bi, hg, ri: (bi, ri, hg)),
        out_shape=jax.ShapeDtypeStruct((bsz, s, D), BF),
        compiler_params=_params(3),
    )(qkv, qkv, qkv, qkv_c, qkv_c, bias_tabs)


def kernel(x, c, ctx, c_ctx, w_mod, b_mod, norm_g, w_ffn_in, w_ffn_out, a_w_pw1, a_b_pw1, a_w_dw, a_b_dw, a_ln_g, a_ln_b, a_w_pw2, a_b_pw2, b_w_in, b_b_in, b_w_conv, b_b_conv, b_w_rg, b_b_rg, b_w_ig, b_b_ig, b_lam, b_w_out, b_b_out, c_w_qkv, c_b_qkv, c_rpb, c_w_o, c_b_o, final_g):
    bsz = x.shape[0]
    cond = jnp.concatenate([c, c_ctx[None], jnp.zeros((SUBLANES - bsz - 1, D), F32)], axis=0)
    mods = _mod_call(cond, w_mod, b_mod)

    def layer_mods(i):
        m = mods[i].reshape(SUBLANES, 6, D)
        pad = ((0, 0), (0, SUBLANES - 6), (0, 0))
        lat = jnp.pad(m[:bsz], pad)
        cx = jnp.pad(jnp.broadcast_to(m[bsz:bsz + 1], (bsz, 6, D)), pad)
        return lat, cx

    def conv_layer(xs, mod, i, j):
        u = _in_proj(xs, mod, norm_g[i, 0], a_w_pw1[j].astype(BF), a_b_pw1[j], "glu")
        return _conv_block(xs, u, mod, a_w_dw[j], a_b_dw[j], a_ln_g[j], a_ln_b[j],
                           a_w_pw2[j].astype(BF), a_b_pw2[j])

    w_ffn_in_bf = w_ffn_in.astype(BF)
    w_ffn_out_bf = w_ffn_out.astype(BF)

    def ffn(xs, mod, i, final_g=None, **proj):
        return _ffn(xs, mod, norm_g[i, 1], w_ffn_in_bf, w_ffn_out_bf, i, final_g, **proj)

    cs = ctx

    lat, cx = layer_mods(0)
    x = ffn(conv_layer(x, lat, 0, 0), lat, 0)
    cs = ffn(conv_layer(cs, cx, 0, 0), cx, 0)

    lat, cx = layer_mods(1)
    w_in = b_w_in[0].astype(BF)
    w_out = b_w_out[0].astype(BF)
    gate_l, xr_l = _lru_in(x, lat, norm_g[1, 0], w_in, b_b_in[0], b_w_conv[0], b_b_conv[0])
    gate_c, xr_c = _lru_in(cs, cx, norm_g[1, 0], w_in, b_b_in[0], b_w_conv[0], b_b_conv[0])
    hs_l, hs_c = [], []
    for d, reverse in ((0, False), (1, True)):
        wg = (0.5 * jnp.concatenate([b_w_rg[0, d], b_w_ig[0, d]], axis=-1)).astype(BF)
        bg = 0.5 * jnp.concatenate([b_b_rg[0, d].reshape(RNN_BLOCKS, 1, RNN_BLOCK),
                                    b_b_ig[0, d].reshape(RNN_BLOCKS, 1, RNN_BLOCK)], axis=-1)
        h0 = jnp.zeros((bsz, 1, D), F32)
        hc, hc_last = _lru_scan(xr_c, h0, wg, bg, b_lam[0, d], reverse)
        hl, _ = _lru_scan(xr_l, hc_last, wg, bg, b_lam[0, d], reverse)
        hs_c.append(hc)
        hs_l.append(hl)
    x = ffn(x, lat, 1, ys=(hs_l[0], hs_l[1], gate_l), w_p=w_out, b_p=b_b_out[0])
    cs = ffn(cs, cx, 1, ys=(hs_c[0], hs_c[1], gate_c), w_p=w_out, b_p=b_b_out[0])

    lat, cx = layer_mods(2)
    w_qkv = c_w_qkv[0].astype(BF)
    qkv = _in_proj(x, lat, norm_g[2, 0], w_qkv, c_b_qkv[0], "qkv")
    qkv_c = _in_proj(cs, cx, norm_g[2, 0], w_qkv, c_b_qkv[0], "qkv")
    o = _na_attention(qkv, qkv_c, _na_bias_tables(c_rpb[0]))
    x = ffn(x, lat, 2, ys=(o,), w_p=c_w_o[0].astype(BF), b_p=c_b_o[0])

    lat, _ = layer_mods(3)
    return ffn(conv_layer(x, lat, 3, 1), lat, 3, final_g)
```

```python
import functools

import jax
import jax.numpy as jnp
import numpy as np
from jax import lax
from jax.experimental import pallas as pl
from jax.experimental.pallas import tpu as pltpu

D = 1024
DEPTH = 4
GRID_W = 64
CTX = 256
CONV_K = 31
RNN_BLOCKS = 8
RNN_BLOCK = 128
RNN_CONV_K = 4
LRU_C = 8.0
HEADS = 16
HD = 64
NA_KH = 8
NA_KW = 16
DFF = 2816
EPS = 1e-6

LANES = 128
SUBLANES = 8
VMEM_LIMIT = 56 * 1024 * 1024
NEG = -1e30

BF = jnp.bfloat16
F32 = jnp.float32


def _dot(a, b):
    return jnp.dot(a, b, preferred_element_type=F32)


def _dot_nt(a, b):
    return lax.dot_general(a, b, (((1,), (1,)), ((), ())), preferred_element_type=F32)


def _params(n_axes):
    return pltpu.CompilerParams(
        dimension_semantics=("arbitrary",) * n_axes, vmem_limit_bytes=VMEM_LIMIT)


def _const_spec(shape):
    nd = len(shape)
    return pl.BlockSpec(shape, lambda *_: (0,) * nd, pipeline_mode=pl.Buffered(1))


def _row_tile(t):
    return min(512, t)


def _normmod(x, g, sc, sh):
    ms = jnp.mean(x * x, axis=-1, keepdims=True)
    return (x * lax.rsqrt(ms + EPS) * g) * (1.0 + sc) + sh


def _sigmoid(x):
    return 0.5 * jnp.tanh(0.5 * x) + 0.5


def _silu(x):
    h = 0.5 * x
    return h * jnp.tanh(h) + h


def _gelu_tanh(x):
    c = 0.7978845608028654
    th = jnp.tanh(x * ((x * x) * (c * 0.044715) + c))
    hx = 0.5 * x
    return hx + hx * th


MOD_TN = 1536


def _mod_kernel(c_ref, w_ref, b_ref, o_ref):
    cnd = c_ref[...]
    s = (cnd * _sigmoid(cnd)).astype(BF)
    o_ref[0] = _dot(s, w_ref[0].astype(BF)) + b_ref[0]


def _mod_call(cond, w_mod, b_mod):
    n = 6 * D
    return pl.pallas_call(
        _mod_kernel,
        grid=(DEPTH, n // MOD_TN),
        in_specs=[
            pl.BlockSpec((SUBLANES, D), lambda i, j: (0, 0)),
            pl.BlockSpec((1, D, MOD_TN), lambda i, j: (i, 0, j)),
            pl.BlockSpec((1, 1, MOD_TN), lambda i, j: (i, 0, j)),
        ],
        out_specs=pl.BlockSpec((1, SUBLANES, MOD_TN), lambda i, j: (i, 0, j)),
        out_shape=jax.ShapeDtypeStruct((DEPTH, SUBLANES, n), F32),
        compiler_params=_params(2),
    )(cond, w_mod, b_mod.reshape(DEPTH, 1, n))


PROJ_CH = 512


def _in_proj_kernel(x_ref, mod_ref, g_ref, w_ref, b_ref, o_ref, *, variant):
    mod = mod_ref[0]
    h = _normmod(x_ref[0], g_ref[...], mod[1:2], mod[0:1]).astype(BF)

    def proj(c0):
        return _dot(h, w_ref[:, c0:c0 + PROJ_CH]) + b_ref[:, c0:c0 + PROJ_CH]

    for j in range(D // PROJ_CH):
        c0 = j * PROJ_CH
        if variant == "glu":
            o_ref[0, :, c0:c0 + PROJ_CH] = proj(c0) * _sigmoid(proj(D + c0))
        else:
            o_ref[0, :, c0:c0 + PROJ_CH] = (proj(c0) * (HD ** -0.5)).astype(BF)
            o_ref[0, :, D + c0:D + c0 + PROJ_CH] = proj(D + c0).astype(BF)
            o_ref[0, :, 2 * D + c0:2 * D + c0 + PROJ_CH] = proj(2 * D + c0).astype(BF)


def _in_proj(x, mod, g, w, b, variant):
    bsz, t, _ = x.shape
    tm = _row_tile(t)
    n = w.shape[1]
    row = lambda width: pl.BlockSpec((1, tm, width), lambda bi, ti: (bi, ti, 0))
    if variant == "glu":
        out_shape, out_spec = jax.ShapeDtypeStruct((bsz, t, D), F32), row(D)
    else:
        out_shape, out_spec = jax.ShapeDtypeStruct((bsz, t, 3 * D), BF), row(3 * D)
    return pl.pallas_call(
        functools.partial(_in_proj_kernel, variant=variant),
        grid=(bsz, t // tm),
        in_specs=[
            row(D),
            pl.BlockSpec((1, SUBLANES, D), lambda bi, ti: (bi, 0, 0)),
            _const_spec((1, D)),
            _const_spec((D, n)),
            _const_spec((1, n)),
        ],
        out_specs=out_spec,
        out_shape=out_shape,
        compiler_params=_params(2),
    )(x, mod, g.reshape(1, D), w, b.reshape(1, n))


CONV_HALO = 16
CONV_RC = 64
CONV_TM = 256
CONV_SPAN = (CONV_K - 1) // SUBLANES * SUBLANES


def _conv_kernel(x_ref, up_ref, uc_ref, un_ref, mod_ref, wdw_ref, bdw_ref, lng_ref, lnb_ref,
                 w2_ref, b2_ref, o_ref, sh_ref, acc_ref, *, tm, nt):
    t = pl.program_id(1)
    for cb in range(D // LANES):
        lanes = slice(cb * LANES, (cb + 1) * LANES)
        sh_ref[0, cb, 0:CONV_HALO, :] = jnp.where(t > 0, up_ref[0, :, lanes], 0.0)
        sh_ref[0, cb, CONV_HALO:CONV_HALO + tm, :] = uc_ref[0, :, lanes]
        sh_ref[0, cb, CONV_HALO + tm:2 * CONV_HALO + tm, :] = jnp.where(
            t < nt - 1, un_ref[0, :, lanes], 0.0)
    span = tm + CONV_SPAN
    for s in range(1, SUBLANES):
        for cb in range(D // LANES):
            sh_ref[s, cb, 0:span, :] = sh_ref[0, cb, s:s + span, :]
    off = CONV_HALO - CONV_K // 2

    def rows(rc, carry):
        base = pl.multiple_of(rc * CONV_RC, CONV_RC)
        for cb in range(D // LANES):
            lanes = slice(cb * LANES, (cb + 1) * LANES)
            acc = jnp.zeros((CONV_RC, LANES), F32)
            for k in range(CONV_K):
                s = (off + k) % SUBLANES
                q = off + k - s
                acc = acc + sh_ref[s, cb, pl.ds(base + q, CONV_RC), :] * wdw_ref[k:k + 1, lanes]
            acc_ref[pl.ds(base, CONV_RC), lanes] = acc + bdw_ref[:, lanes]
        return carry

    lax.fori_loop(0, tm // CONV_RC, rows, 0)

    v = acc_ref[...]
    mu = jnp.mean(v, axis=-1, keepdims=True)
    vc = v - mu
    var = jnp.mean(vc * vc, axis=-1, keepdims=True)
    y = vc * lax.rsqrt(var + EPS) * lng_ref[...] + lnb_ref[...]
    s_act = _silu(y).astype(BF)
    y2 = _dot(s_act, w2_ref[...]) + b2_ref[...]
    o_ref[0] = x_ref[0] + mod_ref[0][2:3] * y2


def _conv_block(x, u, mod, w_dw, b_dw, ln_g, ln_b, w2, b2):
    bsz, t, _ = x.shape
    tm = CONV_TM
    nt = t // tm
    hb = tm // CONV_HALO
    row = pl.BlockSpec((1, tm, D), lambda bi, ti: (bi, ti, 0))
    vec = lambda a: a.reshape(1, D)
    return pl.pallas_call(
        functools.partial(_conv_kernel, tm=tm, nt=nt),
        grid=(bsz, nt),
        in_specs=[
            row,
            pl.BlockSpec((1, CONV_HALO, D), lambda bi, ti: (bi, jnp.maximum(ti * hb - 1, 0), 0)),
            row,
            pl.BlockSpec((1, CONV_HALO, D),
                         lambda bi, ti: (bi, jnp.minimum((ti + 1) * hb, t // CONV_HALO - 1), 0)),
            pl.BlockSpec((1, SUBLANES, D), lambda bi, ti: (bi, 0, 0)),
            _const_spec((CONV_K, D)),
            _const_spec((1, D)),
            _const_spec((1, D)),
            _const_spec((1, D)),
            _const_spec((D, D)),
            _const_spec((1, D)),
        ],
        out_specs=row,
        out_shape=jax.ShapeDtypeStruct((bsz, t, D), F32),
        scratch_shapes=[pltpu.VMEM((SUBLANES, D // LANES, tm + 2 * CONV_HALO, LANES), F32),
                        pltpu.VMEM((tm, D), F32)],
        compiler_params=_params(2),
    )(x, u, u, u, mod, w_dw, vec(b_dw), vec(ln_g), vec(ln_b), w2, vec(b2))


FFN_CH = 256


def _ffn_body(x, mod, g_ref, win_ref, wout_ref, fg_ref, acc_ref, final):
    h = _normmod(x, g_ref[...], mod[4:5], mod[3:4]).astype(BF)
    for j in range(DFF // FFN_CH):
        c0 = j * FFN_CH
        u1 = _dot(h, win_ref[:, c0:c0 + FFN_CH])
        u2 = _dot(h, win_ref[:, DFF + c0:DFF + c0 + FFN_CH])
        a = (_silu(u1) * u2).astype(BF)
        part = _dot(a, wout_ref[c0:c0 + FFN_CH, :])
        if j == 0:
            acc_ref[...] = part
        else:
            acc_ref[...] += part
    out = x + mod[5:6] * acc_ref[...]
    if final:
        ms = jnp.mean(out * out, axis=-1, keepdims=True)
        out = out * lax.rsqrt(ms + EPS) * fg_ref[...]
    return out


def _ffn_kernel(x_ref, mod_ref, *refs, final, n_y):
    y_refs = refs[:n_y]
    if n_y:
        wp_ref, bp_ref = refs[n_y:n_y + 2]
        refs = refs[n_y + 2:]
    g_ref, win_ref, wout_ref, fg_ref, o_ref, acc_ref = refs[:6]
    mod = mod_ref[0]
    if n_y == 0:
        x = x_ref[0]
    else:
        x1_ref = refs[6]
        if n_y == 3:
            y = ((y_refs[0][0] + y_refs[1][0]) * y_refs[2][0]).astype(BF)
        else:
            y = y_refs[0][0]
        x1_ref[...] = x_ref[0] + mod[2:3] * (_dot(y, wp_ref[...]) + bp_ref[...])
        x = x1_ref[...]
    o_ref[0] = _ffn_body(x, mod, g_ref, win_ref, wout_ref, fg_ref, acc_ref, final)


def _ffn(x, mod, g, w_in, w_out, layer, final_g=None, ys=(), w_p=None, b_p=None):
    bsz, t, _ = x.shape
    tm = min(1024, t) if not ys else _row_tile(t)
    final = final_g is not None
    fg = (final_g if final else jnp.ones((D,), F32)).reshape(1, D)
    row = pl.BlockSpec((1, tm, D), lambda bi, ti: (bi, ti, 0))
    layer_spec = lambda r, c: pl.BlockSpec((None, r, c), lambda *_: (layer, 0, 0),
                                           pipeline_mode=pl.Buffered(1))
    n_y = len(ys)
    proj_specs = [row] * n_y + ([_const_spec((D, D)), _const_spec((1, D))] if n_y else [])
    proj_args = list(ys) + ([w_p, b_p.reshape(1, D)] if n_y else [])
    scratch = [pltpu.VMEM((tm, D), F32)] * (2 if n_y else 1)
    return pl.pallas_call(
        functools.partial(_ffn_kernel, final=final, n_y=n_y),
        grid=(bsz, t // tm),
        in_specs=[row, pl.BlockSpec((1, SUBLANES, D), lambda bi, ti: (bi, 0, 0))] + proj_specs + [
            _const_spec((1, D)),
            layer_spec(D, 2 * DFF),
            layer_spec(DFF, D),
            _const_spec((1, D)),
        ],
        out_specs=row,
        out_shape=jax.ShapeDtypeStruct((bsz, t, D), F32),
        scratch_shapes=scratch,
        compiler_params=_params(2),
    )(x, mod, *proj_args, g.reshape(1, D), w_in, w_out, fg)


LRU_HALO = 16


def _lru_in_kernel(xp_ref, x_ref, xn_ref, mod_ref, g_ref, w_ref, b_ref, wc_ref, bc_ref,
                   gate_ref, xr_ref, h_ref, u_ref, *, tm, nt):
    t = pl.program_id(1)
    mod = mod_ref[0]
    g, sc, sh = g_ref[...], mod[1:2], mod[0:1]
    h_ref[0:LRU_HALO, :] = _normmod(xp_ref[0], g, sc, sh).astype(BF)
    h_ref[LRU_HALO:LRU_HALO + tm, :] = _normmod(x_ref[0], g, sc, sh).astype(BF)
    h_ref[LRU_HALO + tm:2 * LRU_HALO + tm, :] = _normmod(xn_ref[0], g, sc, sh).astype(BF)
    for j in range(D // PROJ_CH):
        c0 = j * PROJ_CH
        cols = slice(c0, c0 + PROJ_CH)
        gate_ref[0, :, cols] = _gelu_tanh(
            _dot(h_ref[LRU_HALO:LRU_HALO + tm, :], w_ref[:, cols]) + b_ref[:, cols])
        u = _dot(h_ref[...], w_ref[:, D + c0:D + c0 + PROJ_CH]) + b_ref[:, D + c0:D + c0 + PROJ_CH]
        u_ref[0:LRU_HALO, cols] = jnp.where(t > 0, u[0:LRU_HALO], 0.0)
        u_ref[LRU_HALO:LRU_HALO + tm, cols] = u[LRU_HALO:LRU_HALO + tm]
        u_ref[LRU_HALO + tm:, cols] = jnp.where(t < nt - 1, u[LRU_HALO + tm:], 0.0)
    off = LRU_HALO - RNN_CONV_K // 2
    for n in range(D // LANES):
        lanes = slice(n * LANES, (n + 1) * LANES)
        xr = jnp.zeros((tm, LANES), F32)
        for k in range(RNN_CONV_K):
            xr = xr + u_ref[off + k:off + k + tm, lanes] * wc_ref[k:k + 1, lanes]
        xr_ref[0, :, lanes] = xr + bc_ref[:, lanes]


def _lru_in(x, mod, g, w, b, w_conv, b_conv):
    bsz, t, _ = x.shape
    tm = _row_tile(t)
    nt = t // tm
    hb = tm // LRU_HALO
    row = pl.BlockSpec((1, tm, D), lambda bi, ti: (bi, ti, 0))
    return pl.pallas_call(
        functools.partial(_lru_in_kernel, tm=tm, nt=nt),
        grid=(bsz, nt),
        in_specs=[
            pl.BlockSpec((1, LRU_HALO, D), lambda bi, ti: (bi, jnp.maximum(ti * hb - 1, 0), 0)),
            row,
            pl.BlockSpec((1, LRU_HALO, D),
                         lambda bi, ti: (bi, jnp.minimum((ti + 1) * hb, t // LRU_HALO - 1), 0)),
            pl.BlockSpec((1, SUBLANES, D), lambda bi, ti: (bi, 0, 0)),
            _const_spec((1, D)),
            _const_spec((D, 2 * D)),
            _const_spec((1, 2 * D)),
            _const_spec((RNN_CONV_K, D)),
            _const_spec((1, D)),
        ],
        out_specs=[row, row],
        out_shape=[jax.ShapeDtypeStruct((bsz, t, D), F32)] * 2,
        scratch_shapes=[pltpu.VMEM((tm + 2 * LRU_HALO, D), BF),
                        pltpu.VMEM((tm + 2 * LRU_HALO, D), F32)],
        compiler_params=_params(2),
    )(x, x, x, mod, g.reshape(1, D), w, b.reshape(1, 2 * D), w_conv, b_conv.reshape(1, D))


def _lru_scan_kernel(xr_ref, h0_ref, wg_ref, bg_ref, lam_ref, hs_ref, hl_ref,
                     pc_ref, bs_ref, h_ref, *, tm, reverse):
    t = pl.program_id(1)

    @pl.when(t == 0)
    def _():
        h_ref[...] = jnp.broadcast_to(h0_ref[0], (SUBLANES, D))

    nv = tm // SUBLANES
    sub = lax.broadcasted_iota(jnp.int32, (nv, SUBLANES, LANES), 1)

    for n in range(RNN_BLOCKS):
        lanes = slice(n * LANES, (n + 1) * LANES)
        xr = xr_ref[0, :, lanes]
        gh = _dot(xr.astype(BF), wg_ref[n]) + bg_ref[n]
        tr = jnp.tanh(gh[:, :LANES])
        tg = jnp.tanh(gh[:, LANES:])
        nl = -lam_ref[:, lanes]
        sp = jnp.maximum(nl, 0.0) + jnp.log1p(jnp.exp(-jnp.abs(nl)))
        c1 = (-0.5 * LRU_C) * sp
        log_a = c1 * tr + c1
        a = jnp.exp(log_a)
        th = jnp.tanh(log_a)
        one_m_a2 = (-2.0 * th) / (1.0 - th)
        root = jnp.where(one_m_a2 > 0.0, one_m_a2 * lax.rsqrt(one_m_a2), 0.0)
        hx = 0.5 * xr
        b = root * (hx * tg + hx)
        a3 = a.reshape(nv, SUBLANES, LANES)
        b3 = b.reshape(nv, SUBLANES, LANES)
        for sh in (1, 2, 4):
            if reverse:
                m = sub < SUBLANES - sh
                a_s = pltpu.roll(a3, SUBLANES - sh, 1)
                b_s = pltpu.roll(b3, SUBLANES - sh, 1)
            else:
                m = sub >= sh
                a_s = pltpu.roll(a3, sh, 1)
                b_s = pltpu.roll(b3, sh, 1)
            b3 = jnp.where(m, a3 * b_s + b3, b3)
            a3 = jnp.where(m, a3 * a_s, a3)
        pc_ref[:, lanes] = a3.reshape(tm, LANES)
        bs_ref[:, lanes] = b3.reshape(tm, LANES)

    edge = 0 if reverse else SUBLANES - 1

    def group(i, hcar):
        v = (nv - 1 - i) if reverse else i
        st = pl.multiple_of(v * SUBLANES, SUBLANES)
        res = bs_ref[pl.ds(st, SUBLANES), :] + pc_ref[pl.ds(st, SUBLANES), :] * hcar
        hs_ref[0, pl.ds(st, SUBLANES), :] = res
        return jnp.broadcast_to(res[edge:edge + 1, :], (SUBLANES, D))

    hfin = lax.fori_loop(0, nv, group, h_ref[...])
    h_ref[...] = hfin
    hl_ref[0] = hfin[0:1, :]


def _lru_scan(xr, h0, wg, bg, lam, reverse):
    bsz, t, _ = xr.shape
    tm = _row_tile(t)
    nt = t // tm
    tidx = (lambda ti: nt - 1 - ti) if reverse else (lambda ti: ti)
    row = pl.BlockSpec((1, tm, D), lambda bi, ti: (bi, tidx(ti), 0))
    hs, hl = pl.pallas_call(
        functools.partial(_lru_scan_kernel, tm=tm, reverse=reverse),
        grid=(bsz, nt),
        in_specs=[
            row,
            pl.BlockSpec((1, 1, D), lambda bi, ti: (bi, 0, 0)),
            _const_spec((RNN_BLOCKS, RNN_BLOCK, 2 * RNN_BLOCK)),
            _const_spec((RNN_BLOCKS, 1, 2 * RNN_BLOCK)),
            _const_spec((1, D)),
        ],
        out_specs=[row, pl.BlockSpec((1, 1, D), lambda bi, ti: (bi, 0, 0))],
        out_shape=[jax.ShapeDtypeStruct((bsz, t, D), F32), jax.ShapeDtypeStruct((bsz, 1, D), F32)],
        scratch_shapes=[pltpu.VMEM((tm, D), F32), pltpu.VMEM((tm, D), F32),
                        pltpu.VMEM((SUBLANES, D), F32)],
        compiler_params=_params(2),
    )(xr, h0, wg, bg, lam.reshape(1, D))
    return hs, hl


NA_QR = 4
NA_SR = 12
NA_HG = 4
NA_NQ = NA_QR * GRID_W
NA_QC = NA_NQ
NA_NK = NA_SR * GRID_W
NA_ROWS = 8192 // GRID_W
NA_PAD = NA_KH // 2
NA_CASES = (0, NA_QR, NA_ROWS - NA_QR)


def _na_case_geometry(r):
    rs = min(max(r - NA_KH // 2, 0), NA_ROWS - NA_SR)
    qr = r + np.arange(NA_QR)
    r0 = np.clip(qr - NA_KH // 2, 0, NA_ROWS - NA_KH)
    kr = rs + np.arange(NA_SR)
    row_ok = (kr[None, :] >= r0[:, None]) & (kr[None, :] < r0[:, None] + NA_KH)
    a0 = [NA_PAD + rs - int(q) + NA_KH - 1 for q in qr]
    return row_ok, a0


def _na_table_kernel(cm2_ref, o_ref):
    lo = lax.broadcasted_iota(jnp.int32, (GRID_W, LANES), 1) < GRID_W
    for ci, r in enumerate(NA_CASES):
        row_ok, a0 = _na_case_geometry(r)
        for qi in range(NA_QR):
            for j in range(NA_SR // 2):
                blk = cm2_ref[0, a0[qi] + 2 * j]
                ok_l, ok_r = bool(row_ok[qi, 2 * j]), bool(row_ok[qi, 2 * j + 1])
                if not (ok_l or ok_r):
                    blk = jnp.full((GRID_W, LANES), NEG, F32)
                elif not ok_r:
                    blk = jnp.where(lo, blk, NEG)
                elif not ok_l:
                    blk = jnp.where(lo, NEG, blk)
                o_ref[ci, 0, qi * GRID_W:(qi + 1) * GRID_W, j * LANES:(j + 1) * LANES] = blk


def _na_bias_tables(rpb):
    cols = np.arange(GRID_W)
    c0 = np.clip(cols - NA_KW // 2, 0, GRID_W - NA_KW)
    col_ok = (cols[None, :] >= c0[:, None]) & (cols[None, :] < c0[:, None] + NA_KW)
    col_off = cols[None, :] - cols[:, None] + (NA_KW - 1)
    onehot = ((col_off[..., None] == np.arange(2 * NA_KW - 1)) & col_ok[..., None]).astype(np.float32)
    cm = jnp.einsum("hab,ckb->hack", rpb, jnp.asarray(onehot), precision=lax.Precision.HIGHEST)
    cm = jnp.where(jnp.asarray(col_ok), cm, NEG)
    cm = jnp.pad(cm, ((0, 0), (NA_PAD, NA_PAD), (0, 0), (0, 0)), constant_values=NEG)
    cm2 = jnp.concatenate([cm[:, :-1], cm[:, 1:]], axis=-1)
    na = cm2.shape[1]
    return pl.pallas_call(
        _na_table_kernel,
        grid=(HEADS,),
        in_specs=[pl.BlockSpec((1, na, GRID_W, LANES), lambda h: (h, 0, 0, 0))],
        out_specs=pl.BlockSpec((len(NA_CASES), 1, NA_NQ, NA_NK), lambda h: (0, h, 0, 0)),
        out_shape=jax.ShapeDtypeStruct((len(NA_CASES), HEADS, NA_NQ, NA_NK), F32),
        compiler_params=_params(1),
    )(cm2)


def _na_kernel(q_ref, k_ref, v_ref, kc_ref, vc_ref, bias_ref, o_ref):
    ri = pl.program_id(2)
    rs = jnp.clip(ri * NA_QR - NA_KH // 2, 0, NA_ROWS - NA_SR)
    st = pl.multiple_of(rs * GRID_W, GRID_W)
    lo = lax.broadcasted_iota(jnp.int32, (1, LANES), 1) < HD
    for hp in range(NA_HG // 2):
        lanes = slice(hp * LANES, (hp + 1) * LANES)
        kp = k_ref[0, pl.ds(st, NA_NK), lanes]
        kcp = kc_ref[0, :, lanes]
        vp = jnp.concatenate([v_ref[0, pl.ds(st, NA_NK), lanes], vc_ref[0, :, lanes]], axis=0)
        for qc in range(NA_NQ // NA_QC):
            rows = slice(qc * NA_QC, (qc + 1) * NA_QC)
            qp = q_ref[0, rows, lanes]
            res = []
            for hh in range(2):
                own = lo if hh == 0 else jnp.logical_not(lo)
                qh = jnp.where(own, qp, jnp.zeros_like(qp))
                s = jnp.concatenate([_dot_nt(qh, kp) + bias_ref[0, 2 * hp + hh, rows, :],
                                     _dot_nt(qh, kcp)], axis=-1)
                p = jnp.exp(s - jnp.max(s, axis=-1, keepdims=True)).astype(BF)
                res.append(_dot(p, jnp.where(own, vp, jnp.ones_like(vp))))
            num = jnp.where(lo, res[0], res[1])
            den = pltpu.roll(jnp.where(lo, res[1], res[0]), HD, 1)
            o_ref[0, rows, lanes] = (num / den).astype(BF)


def _na_attention(qkv, qkv_c, bias_tabs):
    bsz, s, _ = qkv.shape
    nr = NA_ROWS // NA_QR
    hw = NA_HG * HD
    nhg = HEADS // NA_HG

    def case(ri):
        return jnp.where(ri == 0, 0, jnp.where(ri == nr - 1, 2, 1))

    return pl.pallas_call(
        _na_kernel,
        grid=(bsz, nhg, nr),
        in_specs=[
            pl.BlockSpec((1, NA_NQ, hw), lambda bi, hg, ri: (bi, ri, hg)),
            pl.BlockSpec((1, s, hw), lambda bi, hg, ri: (bi, 0, nhg + hg)),
            pl.BlockSpec((1, s, hw), lambda bi, hg, ri: (bi, 0, 2 * nhg + hg)),
            pl.BlockSpec((1, CTX, hw), lambda bi, hg, ri: (bi, 0, nhg + hg)),
            pl.BlockSpec((1, CTX, hw), lambda bi, hg, ri: (bi, 0, 2 * nhg + hg)),
            pl.BlockSpec((1, NA_HG, NA_NQ, NA_NK), lambda bi, hg, ri: (case(ri), hg, 0, 0)),
        ],
        out_specs=pl.BlockSpec((1, NA_NQ, hw), lambda bi, hg, ri: (bi, ri, hg)),
        out_shape=jax.ShapeDtypeStruct((bsz, s, D), BF),
        compiler_params=_params(3),
    )(qkv, qkv, qkv, qkv_c, qkv_c, bias_tabs)


def kernel(x, c, ctx, c_ctx, w_mod, b_mod, norm_g, w_ffn_in, w_ffn_out, a_w_pw1, a_b_pw1, a_w_dw, a_b_dw, a_ln_g, a_ln_b, a_w_pw2, a_b_pw2, b_w_in, b_b_in, b_w_conv, b_b_conv, b_w_rg, b_b_rg, b_w_ig, b_b_ig, b_lam, b_w_out, b_b_out, c_w_qkv, c_b_qkv, c_rpb, c_w_o, c_b_o, final_g):
    bsz = x.shape[0]
    cond = jnp.concatenate([c, c_ctx[None], jnp.zeros((SUBLANES - bsz - 1, D), F32)], axis=0)
    mods = _mod_call(cond, w_mod, b_mod)

    def layer_mods(i):
        m = mods[i].reshape(SUBLANES, 6, D)
        pad = ((0, 0), (0, SUBLANES - 6), (0, 0))
        lat = jnp.pad(m[:bsz], pad)
        cx = jnp.pad(jnp.broadcast_to(m[bsz:bsz + 1], (bsz, 6, D)), pad)
        return lat, cx

    def conv_layer(xs, mod, i, j):
        u = _in_proj(xs, mod, norm_g[i, 0], a_w_pw1[j].astype(BF), a_b_pw1[j], "glu")
        return _conv_block(xs, u, mod, a_w_dw[j], a_b_dw[j], a_ln_g[j], a_ln_b[j],
                           a_w_pw2[j].astype(BF), a_b_pw2[j])

    w_ffn_in_bf = w_ffn_in.astype(BF)
    w_ffn_out_bf = w_ffn_out.astype(BF)

    def ffn(xs, mod, i, final_g=None, **proj):
        return _ffn(xs, mod, norm_g[i, 1], w_ffn_in_bf, w_ffn_out_bf, i, final_g, **proj)

    cs = ctx
    flat = lambda a: a.reshape(1, bsz * CTX, a.shape[-1])
    unflat = lambda a: a.reshape(bsz, CTX, a.shape[-1])

    lat, cx = layer_mods(0)
    x = ffn(conv_layer(x, lat, 0, 0), lat, 0)
    u_c = unflat(_in_proj(flat(cs), cx[:1], norm_g[0, 0], a_w_pw1[0].astype(BF), a_b_pw1[0], "glu"))
    cs = _conv_block(cs, u_c, cx, a_w_dw[0], a_b_dw[0], a_ln_g[0], a_ln_b[0],
                     a_w_pw2[0].astype(BF), a_b_pw2[0])
    cs = unflat(ffn(flat(cs), cx[:1], 0))

    lat, cx = layer_mods(1)
    w_in = b_w_in[0].astype(BF)
    w_out = b_w_out[0].astype(BF)
    gate_l, xr_l = _lru_in(x, lat, norm_g[1, 0], w_in, b_b_in[0], b_w_conv[0], b_b_conv[0])
    gate_c, xr_c = _lru_in(cs, cx, norm_g[1, 0], w_in, b_b_in[0], b_w_conv[0], b_b_conv[0])
    hs_l, hs_c = [], []
    for d, reverse in ((0, False), (1, True)):
        wg = (0.5 * jnp.concatenate([b_w_rg[0, d], b_w_ig[0, d]], axis=-1)).astype(BF)
        bg = 0.5 * jnp.concatenate([b_b_rg[0, d].reshape(RNN_BLOCKS, 1, RNN_BLOCK),
                                    b_b_ig[0, d].reshape(RNN_BLOCKS, 1, RNN_BLOCK)], axis=-1)
        h0 = jnp.zeros((bsz, 1, D), F32)
        hc, hc_last = _lru_scan(xr_c, h0, wg, bg, b_lam[0, d], reverse)
        hl, _ = _lru_scan(xr_l, hc_last, wg, bg, b_lam[0, d], reverse)
        hs_c.append(hc)
        hs_l.append(hl)
    x = ffn(x, lat, 1, ys=(hs_l[0], hs_l[1], gate_l), w_p=w_out, b_p=b_b_out[0])
    cs = unflat(ffn(flat(cs), cx[:1], 1, ys=(flat(hs_c[0]), flat(hs_c[1]), flat(gate_c)),
                    w_p=w_out, b_p=b_b_out[0]))

    lat, cx = layer_mods(2)
    w_qkv = c_w_qkv[0].astype(BF)
    qkv = _in_proj(x, lat, norm_g[2, 0], w_qkv, c_b_qkv[0], "qkv")
    qkv_c = unflat(_in_proj(flat(cs), cx[:1], norm_g[2, 0], w_qkv, c_b_qkv[0], "qkv"))
    o = _na_attention(qkv, qkv_c, _na_bias_tables(c_rpb[0]))
    x = ffn(x, lat, 2, ys=(o,), w_p=c_w_o[0].astype(BF), b_p=c_b_o[0])

    lat, _ = layer_mods(3)
    return ffn(conv_layer(x, lat, 3, 1), lat, 3, final_g)
```

```python
import functools

import jax
import jax.numpy as jnp
import numpy as np
from jax import lax
from jax.experimental import pallas as pl
from jax.experimental.pallas import tpu as pltpu

D = 1024
DEPTH = 4
GRID_W = 64
CTX = 256
CONV_K = 31
RNN_BLOCKS = 8
RNN_BLOCK = 128
RNN_CONV_K = 4
LRU_C = 8.0
HEADS = 16
HD = 64
NA_KH = 8
NA_KW = 16
DFF = 2816
EPS = 1e-6

LANES = 128
SUBLANES = 8
VMEM_LIMIT = 56 * 1024 * 1024
NEG = -1e30

BF = jnp.bfloat16
F32 = jnp.float32


def _dot(a, b):
    return jnp.dot(a, b, preferred_element_type=F32)


def _dot_nt(a, b):
    return lax.dot_general(a, b, (((1,), (1,)), ((), ())), preferred_element_type=F32)


def _params(n_axes):
    return pltpu.CompilerParams(
        dimension_semantics=("arbitrary",) * n_axes, vmem_limit_bytes=VMEM_LIMIT)


def _const_spec(shape):
    nd = len(shape)
    return pl.BlockSpec(shape, lambda *_: (0,) * nd, pipeline_mode=pl.Buffered(1))


def _row_tile(t, cap=512):
    return min(cap, t)


def _normmod(x, g, sc, sh):
    ms = jnp.mean(x * x, axis=-1, keepdims=True)
    return (x * lax.rsqrt(ms + EPS) * g) * (1.0 + sc) + sh


def _sigmoid(x):
    return 0.5 * jnp.tanh(0.5 * x) + 0.5


def _silu(x):
    h = 0.5 * x
    return h * jnp.tanh(h) + h


def _gelu_tanh(x):
    c = 0.7978845608028654
    th = jnp.tanh(x * ((x * x) * (c * 0.044715) + c))
    hx = 0.5 * x
    return hx + hx * th


MOD_TN = 1536


def _mod_kernel(c_ref, w_ref, b_ref, o_ref):
    cnd = c_ref[...]
    s = (cnd * _sigmoid(cnd)).astype(BF)
    o_ref[0] = _dot(s, w_ref[0].astype(BF)) + b_ref[0]


def _mod_call(cond, w_mod, b_mod):
    n = 6 * D
    return pl.pallas_call(
        _mod_kernel,
        grid=(DEPTH, n // MOD_TN),
        in_specs=[
            pl.BlockSpec((SUBLANES, D), lambda i, j: (0, 0)),
            pl.BlockSpec((1, D, MOD_TN), lambda i, j: (i, 0, j)),
            pl.BlockSpec((1, 1, MOD_TN), lambda i, j: (i, 0, j)),
        ],
        out_specs=pl.BlockSpec((1, SUBLANES, MOD_TN), lambda i, j: (i, 0, j)),
        out_shape=jax.ShapeDtypeStruct((DEPTH, SUBLANES, n), F32),
        compiler_params=_params(2),
    )(cond, w_mod, b_mod.reshape(DEPTH, 1, n))


PROJ_CH = 512


def _in_proj_kernel(x_ref, mod_ref, g_ref, w_ref, b_ref, o_ref, *, variant):
    mod = mod_ref[0]
    h = _normmod(x_ref[0], g_ref[...], mod[1:2], mod[0:1]).astype(BF)

    def proj(c0):
        return _dot(h, w_ref[:, c0:c0 + PROJ_CH]) + b_ref[:, c0:c0 + PROJ_CH]

    for j in range(D // PROJ_CH):
        c0 = j * PROJ_CH
        if variant == "glu":
            o_ref[0, :, c0:c0 + PROJ_CH] = proj(c0) * _sigmoid(proj(D + c0))
        else:
            o_ref[0, :, c0:c0 + PROJ_CH] = (proj(c0) * (HD ** -0.5)).astype(BF)
            o_ref[0, :, D + c0:D + c0 + PROJ_CH] = proj(D + c0).astype(BF)
            o_ref[0, :, 2 * D + c0:2 * D + c0 + PROJ_CH] = proj(2 * D + c0).astype(BF)


def _in_proj(x, mod, g, w, b, variant):
    bsz, t, _ = x.shape
    tm = _row_tile(t, 1024)
    n = w.shape[1]
    row = lambda width: pl.BlockSpec((1, tm, width), lambda bi, ti: (bi, ti, 0))
    if variant == "glu":
        out_shape, out_spec = jax.ShapeDtypeStruct((bsz, t, D), F32), row(D)
    else:
        out_shape, out_spec = jax.ShapeDtypeStruct((bsz, t, 3 * D), BF), row(3 * D)
    return pl.pallas_call(
        functools.partial(_in_proj_kernel, variant=variant),
        grid=(bsz, t // tm),
        in_specs=[
            row(D),
            pl.BlockSpec((1, SUBLANES, D), lambda bi, ti: (bi, 0, 0)),
            _const_spec((1, D)),
            _const_spec((D, n)),
            _const_spec((1, n)),
        ],
        out_specs=out_spec,
        out_shape=out_shape,
        compiler_params=_params(2),
    )(x, mod, g.reshape(1, D), w, b.reshape(1, n))


CONV_HALO = 16
CONV_RC = 64
CONV_TM = 512
CONV_SPAN = (CONV_K - 1) // SUBLANES * SUBLANES


def _conv_kernel(x_ref, up_ref, uc_ref, un_ref, mod_ref, wdw_ref, bdw_ref, lng_ref, lnb_ref,
                 w2_ref, b2_ref, o_ref, sh_ref, acc_ref, *, tm, nt):
    t = pl.program_id(1)
    for cb in range(D // LANES):
        lanes = slice(cb * LANES, (cb + 1) * LANES)
        sh_ref[0, cb, 0:CONV_HALO, :] = jnp.where(t > 0, up_ref[0, :, lanes], 0.0)
        sh_ref[0, cb, CONV_HALO:CONV_HALO + tm, :] = uc_ref[0, :, lanes]
        sh_ref[0, cb, CONV_HALO + tm:2 * CONV_HALO + tm, :] = jnp.where(
            t < nt - 1, un_ref[0, :, lanes], 0.0)
    span = tm + CONV_SPAN
    for s in range(1, SUBLANES):
        for cb in range(D // LANES):
            sh_ref[s, cb, 0:span, :] = sh_ref[0, cb, s:s + span, :]
    off = CONV_HALO - CONV_K // 2

    def rows(rc, carry):
        base = pl.multiple_of(rc * CONV_RC, CONV_RC)
        for cb in range(D // LANES):
            lanes = slice(cb * LANES, (cb + 1) * LANES)
            acc = jnp.zeros((CONV_RC, LANES), F32)
            for k in range(CONV_K):
                s = (off + k) % SUBLANES
                q = off + k - s
                acc = acc + sh_ref[s, cb, pl.ds(base + q, CONV_RC), :] * wdw_ref[k:k + 1, lanes]
            acc_ref[pl.ds(base, CONV_RC), lanes] = acc + bdw_ref[:, lanes]
        return carry

    lax.fori_loop(0, tm // CONV_RC, rows, 0)

    v = acc_ref[...]
    mu = jnp.mean(v, axis=-1, keepdims=True)
    vc = v - mu
    var = jnp.mean(vc * vc, axis=-1, keepdims=True)
    y = vc * lax.rsqrt(var + EPS) * lng_ref[...] + lnb_ref[...]
    s_act = _silu(y).astype(BF)
    y2 = _dot(s_act, w2_ref[...]) + b2_ref[...]
    o_ref[0] = x_ref[0] + mod_ref[0][2:3] * y2


def _conv_block(x, u, mod, w_dw, b_dw, ln_g, ln_b, w2, b2):
    bsz, t, _ = x.shape
    tm = min(CONV_TM, t)
    nt = t // tm
    hb = tm // CONV_HALO
    row = pl.BlockSpec((1, tm, D), lambda bi, ti: (bi, ti, 0))
    vec = lambda a: a.reshape(1, D)
    return pl.pallas_call(
        functools.partial(_conv_kernel, tm=tm, nt=nt),
        grid=(bsz, nt),
        in_specs=[
            row,
            pl.BlockSpec((1, CONV_HALO, D), lambda bi, ti: (bi, jnp.maximum(ti * hb - 1, 0), 0)),
            row,
            pl.BlockSpec((1, CONV_HALO, D),
                         lambda bi, ti: (bi, jnp.minimum((ti + 1) * hb, t // CONV_HALO - 1), 0)),
            pl.BlockSpec((1, SUBLANES, D), lambda bi, ti: (bi, 0, 0)),
            _const_spec((CONV_K, D)),
            _const_spec((1, D)),
            _const_spec((1, D)),
            _const_spec((1, D)),
            _const_spec((D, D)),
            _const_spec((1, D)),
        ],
        out_specs=row,
        out_shape=jax.ShapeDtypeStruct((bsz, t, D), F32),
        scratch_shapes=[pltpu.VMEM((SUBLANES, D // LANES, tm + 2 * CONV_HALO, LANES), F32),
                        pltpu.VMEM((tm, D), F32)],
        compiler_params=_params(2),
    )(x, u, u, u, mod, w_dw, vec(b_dw), vec(ln_g), vec(ln_b), w2, vec(b2))


FFN_CH = 256


def _ffn_body(x, mod, g_ref, win_ref, wout_ref, fg_ref, acc_ref, final):
    h = _normmod(x, g_ref[...], mod[4:5], mod[3:4]).astype(BF)
    for j in range(DFF // FFN_CH):
        c0 = j * FFN_CH
        u1 = _dot(h, win_ref[:, c0:c0 + FFN_CH])
        u2 = _dot(h, win_ref[:, DFF + c0:DFF + c0 + FFN_CH])
        a = (_silu(u1) * u2).astype(BF)
        part = _dot(a, wout_ref[c0:c0 + FFN_CH, :])
        if j == 0:
            acc_ref[...] = part
        else:
            acc_ref[...] += part
    out = x + mod[5:6] * acc_ref[...]
    if final:
        ms = jnp.mean(out * out, axis=-1, keepdims=True)
        out = out * lax.rsqrt(ms + EPS) * fg_ref[...]
    return out


def _ffn_kernel(x_ref, mod_ref, *refs, final, n_y):
    y_refs = refs[:n_y]
    if n_y:
        wp_ref, bp_ref = refs[n_y:n_y + 2]
        refs = refs[n_y + 2:]
    g_ref, win_ref, wout_ref, fg_ref, o_ref, acc_ref = refs[:6]
    mod = mod_ref[0]
    if n_y == 0:
        x = x_ref[0]
    else:
        x1_ref = refs[6]
        if n_y == 3:
            y = ((y_refs[0][0] + y_refs[1][0]) * y_refs[2][0]).astype(BF)
        else:
            y = y_refs[0][0]
        x1_ref[...] = x_ref[0] + mod[2:3] * (_dot(y, wp_ref[...]) + bp_ref[...])
        x = x1_ref[...]
    o_ref[0] = _ffn_body(x, mod, g_ref, win_ref, wout_ref, fg_ref, acc_ref, final)


def _ffn(x, mod, g, w_in, w_out, layer, final_g=None, ys=(), w_p=None, b_p=None):
    bsz, t, _ = x.shape
    tm = min(1024, t) if not ys else _row_tile(t)
    final = final_g is not None
    fg = (final_g if final else jnp.ones((D,), F32)).reshape(1, D)
    row = pl.BlockSpec((1, tm, D), lambda bi, ti: (bi, ti, 0))
    layer_spec = lambda r, c: pl.BlockSpec((None, r, c), lambda *_: (layer, 0, 0),
                                           pipeline_mode=pl.Buffered(1))
    n_y = len(ys)
    proj_specs = [row] * n_y + ([_const_spec((D, D)), _const_spec((1, D))] if n_y else [])
    proj_args = list(ys) + ([w_p, b_p.reshape(1, D)] if n_y else [])
    scratch = [pltpu.VMEM((tm, D), F32)] * (2 if n_y else 1)
    return pl.pallas_call(
        functools.partial(_ffn_kernel, final=final, n_y=n_y),
        grid=(bsz, t // tm),
        in_specs=[row, pl.BlockSpec((1, SUBLANES, D), lambda bi, ti: (bi, 0, 0))] + proj_specs + [
            _const_spec((1, D)),
            layer_spec(D, 2 * DFF),
            layer_spec(DFF, D),
            _const_spec((1, D)),
        ],
        out_specs=row,
        out_shape=jax.ShapeDtypeStruct((bsz, t, D), F32),
        scratch_shapes=scratch,
        compiler_params=_params(2),
    )(x, mod, *proj_args, g.reshape(1, D), w_in, w_out, fg)


LRU_HALO = 16


def _lru_in_kernel(xp_ref, x_ref, xn_ref, mod_ref, g_ref, w_ref, b_ref, wc_ref, bc_ref,
                   gate_ref, xr_ref, h_ref, u_ref, *, tm, nt):
    t = pl.program_id(1)
    mod = mod_ref[0]
    g, sc, sh = g_ref[...], mod[1:2], mod[0:1]
    h_ref[0:LRU_HALO, :] = _normmod(xp_ref[0], g, sc, sh).astype(BF)
    h_ref[LRU_HALO:LRU_HALO + tm, :] = _normmod(x_ref[0], g, sc, sh).astype(BF)
    h_ref[LRU_HALO + tm:2 * LRU_HALO + tm, :] = _normmod(xn_ref[0], g, sc, sh).astype(BF)
    for j in range(D // PROJ_CH):
        c0 = j * PROJ_CH
        cols = slice(c0, c0 + PROJ_CH)
        gate_ref[0, :, cols] = _gelu_tanh(
            _dot(h_ref[LRU_HALO:LRU_HALO + tm, :], w_ref[:, cols]) + b_ref[:, cols])
        u = _dot(h_ref[...], w_ref[:, D + c0:D + c0 + PROJ_CH]) + b_ref[:, D + c0:D + c0 + PROJ_CH]
        u_ref[0:LRU_HALO, cols] = jnp.where(t > 0, u[0:LRU_HALO], 0.0)
        u_ref[LRU_HALO:LRU_HALO + tm, cols] = u[LRU_HALO:LRU_HALO + tm]
        u_ref[LRU_HALO + tm:, cols] = jnp.where(t < nt - 1, u[LRU_HALO + tm:], 0.0)
    off = LRU_HALO - RNN_CONV_K // 2
    for n in range(D // LANES):
        lanes = slice(n * LANES, (n + 1) * LANES)
        xr = jnp.zeros((tm, LANES), F32)
        for k in range(RNN_CONV_K):
            xr = xr + u_ref[off + k:off + k + tm, lanes] * wc_ref[k:k + 1, lanes]
        xr_ref[0, :, lanes] = xr + bc_ref[:, lanes]


def _lru_in(x, mod, g, w, b, w_conv, b_conv):
    bsz, t, _ = x.shape
    tm = _row_tile(t)
    nt = t // tm
    hb = tm // LRU_HALO
    row = pl.BlockSpec((1, tm, D), lambda bi, ti: (bi, ti, 0))
    return pl.pallas_call(
        functools.partial(_lru_in_kernel, tm=tm, nt=nt),
        grid=(bsz, nt),
        in_specs=[
            pl.BlockSpec((1, LRU_HALO, D), lambda bi, ti: (bi, jnp.maximum(ti * hb - 1, 0), 0)),
            row,
            pl.BlockSpec((1, LRU_HALO, D),
                         lambda bi, ti: (bi, jnp.minimum((ti + 1) * hb, t // LRU_HALO - 1), 0)),
            pl.BlockSpec((1, SUBLANES, D), lambda bi, ti: (bi, 0, 0)),
            _const_spec((1, D)),
            _const_spec((D, 2 * D)),
            _const_spec((1, 2 * D)),
            _const_spec((RNN_CONV_K, D)),
            _const_spec((1, D)),
        ],
        out_specs=[row, row],
        out_shape=[jax.ShapeDtypeStruct((bsz, t, D), F32)] * 2,
        scratch_shapes=[pltpu.VMEM((tm + 2 * LRU_HALO, D), BF),
                        pltpu.VMEM((tm + 2 * LRU_HALO, D), F32)],
        compiler_params=_params(2),
    )(x, x, x, mod, g.reshape(1, D), w, b.reshape(1, 2 * D), w_conv, b_conv.reshape(1, D))


def _lru_scan_kernel(xr_ref, h0_ref, wg_ref, bg_ref, lam_ref, hs_ref, hl_ref,
                     pc_ref, bs_ref, h_ref, *, tm, reverse):
    t = pl.program_id(1)

    @pl.when(t == 0)
    def _():
        h_ref[...] = jnp.broadcast_to(h0_ref[0], (SUBLANES, D))

    nv = tm // SUBLANES
    sub = lax.broadcasted_iota(jnp.int32, (nv, SUBLANES, LANES), 1)

    for n in range(RNN_BLOCKS):
        lanes = slice(n * LANES, (n + 1) * LANES)
        xr = xr_ref[0, :, lanes]
        gh = _dot(xr.astype(BF), wg_ref[n]) + bg_ref[n]
        tr = jnp.tanh(gh[:, :LANES])
        tg = jnp.tanh(gh[:, LANES:])
        nl = -lam_ref[:, lanes]
        sp = jnp.maximum(nl, 0.0) + jnp.log1p(jnp.exp(-jnp.abs(nl)))
        c1 = (-0.5 * LRU_C) * sp
        log_a = c1 * tr + c1
        a = jnp.exp(log_a)
        th = jnp.tanh(log_a)
        one_m_a2 = (-2.0 * th) / (1.0 - th)
        root = jnp.where(one_m_a2 > 0.0, one_m_a2 * lax.rsqrt(one_m_a2), 0.0)
        hx = 0.5 * xr
        b = root * (hx * tg + hx)
        a3 = a.reshape(nv, SUBLANES, LANES)
        b3 = b.reshape(nv, SUBLANES, LANES)
        for sh in (1, 2, 4):
            if reverse:
                m = sub < SUBLANES - sh
                a_s = pltpu.roll(a3, SUBLANES - sh, 1)
                b_s = pltpu.roll(b3, SUBLANES - sh, 1)
            else:
                m = sub >= sh
                a_s = pltpu.roll(a3, sh, 1)
                b_s = pltpu.roll(b3, sh, 1)
            b3 = jnp.where(m, a3 * b_s + b3, b3)
            a3 = jnp.where(m, a3 * a_s, a3)
        pc_ref[:, lanes] = a3.reshape(tm, LANES)
        bs_ref[:, lanes] = b3.reshape(tm, LANES)

    edge = 0 if reverse else SUBLANES - 1

    def group(i, hcar):
        v = (nv - 1 - i) if reverse else i
        st = pl.multiple_of(v * SUBLANES, SUBLANES)
        res = bs_ref[pl.ds(st, SUBLANES), :] + pc_ref[pl.ds(st, SUBLANES), :] * hcar
        hs_ref[0, pl.ds(st, SUBLANES), :] = res
        return jnp.broadcast_to(res[edge:edge + 1, :], (SUBLANES, D))

    hfin = lax.fori_loop(0, nv, group, h_ref[...])
    h_ref[...] = hfin
    hl_ref[0] = hfin[0:1, :]


def _lru_scan(xr, h0, wg, bg, lam, reverse):
    bsz, t, _ = xr.shape
    tm = _row_tile(t, 1024)
    nt = t // tm
    tidx = (lambda ti: nt - 1 - ti) if reverse else (lambda ti: ti)
    row = pl.BlockSpec((1, tm, D), lambda bi, ti: (bi, tidx(ti), 0))
    hs, hl = pl.pallas_call(
        functools.partial(_lru_scan_kernel, tm=tm, reverse=reverse),
        grid=(bsz, nt),
        in_specs=[
            row,
            pl.BlockSpec((1, 1, D), lambda bi, ti: (bi, 0, 0)),
            _const_spec((RNN_BLOCKS, RNN_BLOCK, 2 * RNN_BLOCK)),
            _const_spec((RNN_BLOCKS, 1, 2 * RNN_BLOCK)),
            _const_spec((1, D)),
        ],
        out_specs=[row, pl.BlockSpec((1, 1, D), lambda bi, ti: (bi, 0, 0))],
        out_shape=[jax.ShapeDtypeStruct((bsz, t, D), F32), jax.ShapeDtypeStruct((bsz, 1, D), F32)],
        scratch_shapes=[pltpu.VMEM((tm, D), F32), pltpu.VMEM((tm, D), F32),
                        pltpu.VMEM((SUBLANES, D), F32)],
        compiler_params=_params(2),
    )(xr, h0, wg, bg, lam.reshape(1, D))
    return hs, hl


NA_QR = 4
NA_SR = 12
NA_HG = 4
NA_NQ = NA_QR * GRID_W
NA_QC = NA_NQ
NA_NK = NA_SR * GRID_W
NA_ROWS = 8192 // GRID_W
NA_PAD = NA_KH // 2
NA_CASES = (0, NA_QR, NA_ROWS - NA_QR)


def _na_case_geometry(r):
    rs = min(max(r - NA_KH // 2, 0), NA_ROWS - NA_SR)
    qr = r + np.arange(NA_QR)
    r0 = np.clip(qr - NA_KH // 2, 0, NA_ROWS - NA_KH)
    kr = rs + np.arange(NA_SR)
    row_ok = (kr[None, :] >= r0[:, None]) & (kr[None, :] < r0[:, None] + NA_KH)
    a0 = [NA_PAD + rs - int(q) + NA_KH - 1 for q in qr]
    return row_ok, a0


def _na_table_kernel(cm2_ref, o_ref):
    lo = lax.broadcasted_iota(jnp.int32, (GRID_W, LANES), 1) < GRID_W
    for ci, r in enumerate(NA_CASES):
        row_ok, a0 = _na_case_geometry(r)
        for qi in range(NA_QR):
            for j in range(NA_SR // 2):
                blk = cm2_ref[0, a0[qi] + 2 * j]
                ok_l, ok_r = bool(row_ok[qi, 2 * j]), bool(row_ok[qi, 2 * j + 1])
                if not (ok_l or ok_r):
                    blk = jnp.full((GRID_W, LANES), NEG, F32)
                elif not ok_r:
                    blk = jnp.where(lo, blk, NEG)
                elif not ok_l:
                    blk = jnp.where(lo, NEG, blk)
                o_ref[ci, 0, qi * GRID_W:(qi + 1) * GRID_W, j * LANES:(j + 1) * LANES] = blk


def _na_bias_tables(rpb):
    cols = np.arange(GRID_W)
    c0 = np.clip(cols - NA_KW // 2, 0, GRID_W - NA_KW)
    col_ok = (cols[None, :] >= c0[:, None]) & (cols[None, :] < c0[:, None] + NA_KW)
    col_off = cols[None, :] - cols[:, None] + (NA_KW - 1)
    onehot = ((col_off[..., None] == np.arange(2 * NA_KW - 1)) & col_ok[..., None]).astype(np.float32)
    cm = jnp.einsum("hab,ckb->hack", rpb, jnp.asarray(onehot), precision=lax.Precision.HIGHEST)
    cm = jnp.where(jnp.asarray(col_ok), cm, NEG)
    cm = jnp.pad(cm, ((0, 0), (NA_PAD, NA_PAD), (0, 0), (0, 0)), constant_values=NEG)
    cm2 = jnp.concatenate([cm[:, :-1], cm[:, 1:]], axis=-1)
    na = cm2.shape[1]
    return pl.pallas_call(
        _na_table_kernel,
        grid=(HEADS,),
        in_specs=[pl.BlockSpec((1, na, GRID_W, LANES), lambda h: (h, 0, 0, 0))],
        out_specs=pl.BlockSpec((len(NA_CASES), 1, NA_NQ, NA_NK), lambda h: (0, h, 0, 0)),
        out_shape=jax.ShapeDtypeStruct((len(NA_CASES), HEADS, NA_NQ, NA_NK), F32),
        compiler_params=_params(1),
    )(cm2)


def _na_kernel(q_ref, k_ref, v_ref, kc_ref, vc_ref, bias_ref, o_ref):
    ri = pl.program_id(2)
    rs = jnp.clip(ri * NA_QR - NA_KH // 2, 0, NA_ROWS - NA_SR)
    st = pl.multiple_of(rs * GRID_W, GRID_W)
    lo = lax.broadcasted_iota(jnp.int32, (1, LANES), 1) < HD
    for hp in range(NA_HG // 2):
        lanes = slice(hp * LANES, (hp + 1) * LANES)
        kp = k_ref[0, pl.ds(st, NA_NK), lanes]
        kcp = kc_ref[0, :, lanes]
        vp = jnp.concatenate([v_ref[0, pl.ds(st, NA_NK), lanes], vc_ref[0, :, lanes]], axis=0)
        for qc in range(NA_NQ // NA_QC):
            rows = slice(qc * NA_QC, (qc + 1) * NA_QC)
            qp = q_ref[0, rows, lanes]
            res = []
            for hh in range(2):
                own = lo if hh == 0 else jnp.logical_not(lo)
                qh = jnp.where(own, qp, jnp.zeros_like(qp))
                s = jnp.concatenate([_dot_nt(qh, kp) + bias_ref[0, 2 * hp + hh, rows, :],
                                     _dot_nt(qh, kcp)], axis=-1)
                p = jnp.exp(s - jnp.max(s, axis=-1, keepdims=True)).astype(BF)
                res.append(_dot(p, jnp.where(own, vp, jnp.ones_like(vp))))
            num = jnp.where(lo, res[0], res[1])
            den = pltpu.roll(jnp.where(lo, res[1], res[0]), HD, 1)
            o_ref[0, rows, lanes] = (num / den).astype(BF)


def _na_attention(qkv, qkv_c, bias_tabs):
    bsz, s, _ = qkv.shape
    nr = NA_ROWS // NA_QR
    hw = NA_HG * HD
    nhg = HEADS // NA_HG

    def case(ri):
        return jnp.where(ri == 0, 0, jnp.where(ri == nr - 1, 2, 1))

    return pl.pallas_call(
        _na_kernel,
        grid=(bsz, nhg, nr),
        in_specs=[
            pl.BlockSpec((1, NA_NQ, hw), lambda bi, hg, ri: (bi, ri, hg)),
            pl.BlockSpec((1, s, hw), lambda bi, hg, ri: (bi, 0, nhg + hg)),
            pl.BlockSpec((1, s, hw), lambda bi, hg, ri: (bi, 0, 2 * nhg + hg)),
            pl.BlockSpec((1, CTX, hw), lambda bi, hg, ri: (bi, 0, nhg + hg)),
            pl.BlockSpec((1, CTX, hw), lambda bi, hg, ri: (bi, 0, 2 * nhg + hg)),
            pl.BlockSpec((1, NA_HG, NA_NQ, NA_NK), lambda bi, hg, ri: (case(ri), hg, 0, 0)),
        ],
        out_specs=pl.BlockSpec((1, NA_NQ, hw), lambda bi, hg, ri: (bi, ri, hg)),
        out_shape=jax.ShapeDtypeStruct((bsz, s, D), BF),
        compiler_params=_params(3),
    )(qkv, qkv, qkv, qkv_c, qkv_c, bias_tabs)


def kernel(x, c, ctx, c_ctx, w_mod, b_mod, norm_g, w_ffn_in, w_ffn_out, a_w_pw1, a_b_pw1, a_w_dw, a_b_dw, a_ln_g, a_ln_b, a_w_pw2, a_b_pw2, b_w_in, b_b_in, b_w_conv, b_b_conv, b_w_rg, b_b_rg, b_w_ig, b_b_ig, b_lam, b_w_out, b_b_out, c_w_qkv, c_b_qkv, c_rpb, c_w_o, c_b_o, final_g):
    bsz = x.shape[0]
    cond = jnp.concatenate([c, c_ctx[None], jnp.zeros((SUBLANES - bsz - 1, D), F32)], axis=0)
    mods = _mod_call(cond, w_mod, b_mod)

    def layer_mods(i):
        m = mods[i].reshape(SUBLANES, 6, D)
        pad = ((0, 0), (0, SUBLANES - 6), (0, 0))
        lat = jnp.pad(m[:bsz], pad)
        cx = jnp.pad(jnp.broadcast_to(m[bsz:bsz + 1], (bsz, 6, D)), pad)
        return lat, cx

    def conv_layer(xs, mod, i, j):
        u = _in_proj(xs, mod, norm_g[i, 0], a_w_pw1[j].astype(BF), a_b_pw1[j], "glu")
        return _conv_block(xs, u, mod, a_w_dw[j], a_b_dw[j], a_ln_g[j], a_ln_b[j],
                           a_w_pw2[j].astype(BF), a_b_pw2[j])

    w_ffn_in_bf = w_ffn_in.astype(BF)
    w_ffn_out_bf = w_ffn_out.astype(BF)

    def ffn(xs, mod, i, final_g=None, **proj):
        return _ffn(xs, mod, norm_g[i, 1], w_ffn_in_bf, w_ffn_out_bf, i, final_g, **proj)

    cs = ctx
    flat = lambda a: a.reshape(1, bsz * CTX, a.shape[-1])
    unflat = lambda a: a.reshape(bsz, CTX, a.shape[-1])

    lat, cx = layer_mods(0)
    x = ffn(conv_layer(x, lat, 0, 0), lat, 0)
    u_c = unflat(_in_proj(flat(cs), cx[:1], norm_g[0, 0], a_w_pw1[0].astype(BF), a_b_pw1[0], "glu"))
    cs = _conv_block(cs, u_c, cx, a_w_dw[0], a_b_dw[0], a_ln_g[0], a_ln_b[0],
                     a_w_pw2[0].astype(BF), a_b_pw2[0])
    cs = unflat(ffn(flat(cs), cx[:1], 0))

    lat, cx = layer_mods(1)
    w_in = b_w_in[0].astype(BF)
    w_out = b_w_out[0].astype(BF)
    gate_l, xr_l = _lru_in(x, lat, norm_g[1, 0], w_in, b_b_in[0], b_w_conv[0], b_b_conv[0])
    gate_c, xr_c = _lru_in(cs, cx, norm_g[1, 0], w_in, b_b_in[0], b_w_conv[0], b_b_conv[0])
    hs_l, hs_c = [], []
    for d, reverse in ((0, False), (1, True)):
        wg = (0.5 * jnp.concatenate([b_w_rg[0, d], b_w_ig[0, d]], axis=-1)).astype(BF)
        bg = 0.5 * jnp.concatenate([b_b_rg[0, d].reshape(RNN_BLOCKS, 1, RNN_BLOCK),
                                    b_b_ig[0, d].reshape(RNN_BLOCKS, 1, RNN_BLOCK)], axis=-1)
        h0 = jnp.zeros((bsz, 1, D), F32)
        hc, hc_last = _lru_scan(xr_c, h0, wg, bg, b_lam[0, d], reverse)
        hl, _ = _lru_scan(xr_l, hc_last, wg, bg, b_lam[0, d], reverse)
        hs_c.append(hc)
        hs_l.append(hl)
    x = ffn(x, lat, 1, ys=(hs_l[0], hs_l[1], gate_l), w_p=w_out, b_p=b_b_out[0])
    cs = unflat(ffn(flat(cs), cx[:1], 1, ys=(flat(hs_c[0]), flat(hs_c[1]), flat(gate_c)),
                    w_p=w_out, b_p=b_b_out[0]))

    lat, cx = layer_mods(2)
    w_qkv = c_w_qkv[0].astype(BF)
    qkv = _in_proj(x, lat, norm_g[2, 0], w_qkv, c_b_qkv[0], "qkv")
    qkv_c = unflat(_in_proj(flat(cs), cx[:1], norm_g[2, 0], w_qkv, c_b_qkv[0], "qkv"))
    o = _na_attention(qkv, qkv_c, _na_bias_tables(c_rpb[0]))
    x = ffn(x, lat, 2, ys=(o,), w_p=c_w_o[0].astype(BF), b_p=c_b_o[0])

    lat, _ = layer_mods(3)
    return ffn(conv_layer(x, lat, 3, 1), lat, 3, final_g)
```

```python
import functools

import jax
import jax.numpy as jnp
import numpy as np
from jax import lax
from jax.experimental import pallas as pl
from jax.experimental.pallas import tpu as pltpu

D = 1024
DEPTH = 4
GRID_W = 64
CTX = 256
CONV_K = 31
RNN_BLOCKS = 8
RNN_BLOCK = 128
RNN_CONV_K = 4
LRU_C = 8.0
HEADS = 16
HD = 64
NA_KH = 8
NA_KW = 16
DFF = 2816
EPS = 1e-6

LANES = 128
SUBLANES = 8
VMEM_LIMIT = 56 * 1024 * 1024
NEG = -1e30

BF = jnp.bfloat16
F32 = jnp.float32


def _dot(a, b):
    return jnp.dot(a, b, preferred_element_type=F32)


def _dot_nt(a, b):
    return lax.dot_general(a, b, (((1,), (1,)), ((), ())), preferred_element_type=F32)


def _params(n_axes):
    return pltpu.CompilerParams(
        dimension_semantics=("arbitrary",) * n_axes, vmem_limit_bytes=VMEM_LIMIT)


def _const_spec(shape):
    nd = len(shape)
    return pl.BlockSpec(shape, lambda *_: (0,) * nd, pipeline_mode=pl.Buffered(1))


def _row_tile(t, cap=512):
    return min(cap, t)


def _normmod(x, g, sc, sh):
    ms = jnp.mean(x * x, axis=-1, keepdims=True)
    return (x * lax.rsqrt(ms + EPS) * g) * (1.0 + sc) + sh


def _sigmoid(x):
    return 0.5 * jnp.tanh(0.5 * x) + 0.5


def _silu(x):
    h = 0.5 * x
    return h * jnp.tanh(h) + h


def _gelu_tanh(x):
    c = 0.7978845608028654
    th = jnp.tanh(x * ((x * x) * (c * 0.044715) + c))
    hx = 0.5 * x
    return hx + hx * th


MOD_TN = 1536


def _mod_kernel(c_ref, w_ref, b_ref, o_ref):
    cnd = c_ref[...]
    s = (cnd * _sigmoid(cnd)).astype(BF)
    o_ref[0] = _dot(s, w_ref[0].astype(BF)) + b_ref[0]


def _mod_call(cond, w_mod, b_mod):
    n = 6 * D
    return pl.pallas_call(
        _mod_kernel,
        grid=(DEPTH, n // MOD_TN),
        in_specs=[
            pl.BlockSpec((SUBLANES, D), lambda i, j: (0, 0)),
            pl.BlockSpec((1, D, MOD_TN), lambda i, j: (i, 0, j)),
            pl.BlockSpec((1, 1, MOD_TN), lambda i, j: (i, 0, j)),
        ],
        out_specs=pl.BlockSpec((1, SUBLANES, MOD_TN), lambda i, j: (i, 0, j)),
        out_shape=jax.ShapeDtypeStruct((DEPTH, SUBLANES, n), F32),
        compiler_params=_params(2),
    )(cond, w_mod, b_mod.reshape(DEPTH, 1, n))


PROJ_CH = 512


def _in_proj_kernel(x_ref, mod_ref, g_ref, w_ref, b_ref, o_ref, *, variant):
    mod = mod_ref[0]
    h = _normmod(x_ref[0], g_ref[...], mod[1:2], mod[0:1]).astype(BF)

    def proj(c0):
        return _dot(h, w_ref[:, c0:c0 + PROJ_CH]) + b_ref[:, c0:c0 + PROJ_CH]

    for j in range(D // PROJ_CH):
        c0 = j * PROJ_CH
        if variant == "glu":
            o_ref[0, :, c0:c0 + PROJ_CH] = proj(c0) * _sigmoid(proj(D + c0))
        else:
            o_ref[0, :, c0:c0 + PROJ_CH] = (proj(c0) * (HD ** -0.5)).astype(BF)
            o_ref[0, :, D + c0:D + c0 + PROJ_CH] = proj(D + c0).astype(BF)
            o_ref[0, :, 2 * D + c0:2 * D + c0 + PROJ_CH] = proj(2 * D + c0).astype(BF)


def _in_proj(x, mod, g, w, b, variant):
    bsz, t, _ = x.shape
    tm = _row_tile(t, 1024)
    n = w.shape[1]
    row = lambda width: pl.BlockSpec((1, tm, width), lambda bi, ti: (bi, ti, 0))
    if variant == "glu":
        out_shape, out_spec = jax.ShapeDtypeStruct((bsz, t, D), F32), row(D)
    else:
        out_shape, out_spec = jax.ShapeDtypeStruct((bsz, t, 3 * D), BF), row(3 * D)
    return pl.pallas_call(
        functools.partial(_in_proj_kernel, variant=variant),
        grid=(bsz, t // tm),
        in_specs=[
            row(D),
            pl.BlockSpec((1, SUBLANES, D), lambda bi, ti: (bi, 0, 0)),
            _const_spec((1, D)),
            _const_spec((D, n)),
            _const_spec((1, n)),
        ],
        out_specs=out_spec,
        out_shape=out_shape,
        compiler_params=_params(2),
    )(x, mod, g.reshape(1, D), w, b.reshape(1, n))


CONV_HALO = 16
CONV_RC = 64
CONV_TM = 512
CONV_SPAN = (CONV_K - 1) // SUBLANES * SUBLANES


def _conv_kernel(x_ref, up_ref, uc_ref, un_ref, mod_ref, wdw_ref, bdw_ref, lng_ref, lnb_ref,
                 w2_ref, b2_ref, o_ref, sh_ref, acc_ref, *, tm, nt):
    t = pl.program_id(1)
    for cb in range(D // LANES):
        lanes = slice(cb * LANES, (cb + 1) * LANES)
        sh_ref[0, cb, 0:CONV_HALO, :] = jnp.where(t > 0, up_ref[0, :, lanes], 0.0)
        sh_ref[0, cb, CONV_HALO:CONV_HALO + tm, :] = uc_ref[0, :, lanes]
        sh_ref[0, cb, CONV_HALO + tm:2 * CONV_HALO + tm, :] = jnp.where(
            t < nt - 1, un_ref[0, :, lanes], 0.0)
    span = tm + CONV_SPAN
    for s in range(1, SUBLANES):
        for cb in range(D // LANES):
            sh_ref[s, cb, 0:span, :] = sh_ref[0, cb, s:s + span, :]
    off = CONV_HALO - CONV_K // 2

    def rows(rc, carry):
        base = pl.multiple_of(rc * CONV_RC, CONV_RC)
        for cb in range(D // LANES):
            lanes = slice(cb * LANES, (cb + 1) * LANES)
            acc = jnp.zeros((CONV_RC, LANES), F32)
            for k in range(CONV_K):
                s = (off + k) % SUBLANES
                q = off + k - s
                acc = acc + sh_ref[s, cb, pl.ds(base + q, CONV_RC), :] * wdw_ref[k:k + 1, lanes]
            acc_ref[pl.ds(base, CONV_RC), lanes] = acc + bdw_ref[:, lanes]
        return carry

    lax.fori_loop(0, tm // CONV_RC, rows, 0)

    v = acc_ref[...]
    mu = jnp.mean(v, axis=-1, keepdims=True)
    vc = v - mu
    var = jnp.mean(vc * vc, axis=-1, keepdims=True)
    y = vc * lax.rsqrt(var + EPS) * lng_ref[...] + lnb_ref[...]
    s_act = _silu(y).astype(BF)
    y2 = _dot(s_act, w2_ref[...]) + b2_ref[...]
    o_ref[0] = x_ref[0] + mod_ref[0][2:3] * y2


def _conv_block(x, u, mod, w_dw, b_dw, ln_g, ln_b, w2, b2):
    bsz, t, _ = x.shape
    tm = min(CONV_TM, t)
    nt = t // tm
    hb = tm // CONV_HALO
    row = pl.BlockSpec((1, tm, D), lambda bi, ti: (bi, ti, 0))
    vec = lambda a: a.reshape(1, D)
    return pl.pallas_call(
        functools.partial(_conv_kernel, tm=tm, nt=nt),
        grid=(bsz, nt),
        in_specs=[
            row,
            pl.BlockSpec((1, CONV_HALO, D), lambda bi, ti: (bi, jnp.maximum(ti * hb - 1, 0), 0)),
            row,
            pl.BlockSpec((1, CONV_HALO, D),
                         lambda bi, ti: (bi, jnp.minimum((ti + 1) * hb, t // CONV_HALO - 1), 0)),
            pl.BlockSpec((1, SUBLANES, D), lambda bi, ti: (bi, 0, 0)),
            _const_spec((CONV_K, D)),
            _const_spec((1, D)),
            _const_spec((1, D)),
            _const_spec((1, D)),
            _const_spec((D, D)),
            _const_spec((1, D)),
        ],
        out_specs=row,
        out_shape=jax.ShapeDtypeStruct((bsz, t, D), F32),
        scratch_shapes=[pltpu.VMEM((SUBLANES, D // LANES, tm + 2 * CONV_HALO, LANES), F32),
                        pltpu.VMEM((tm, D), F32)],
        compiler_params=_params(2),
    )(x, u, u, u, mod, w_dw, vec(b_dw), vec(ln_g), vec(ln_b), w2, vec(b2))


FFN_CH = 256


def _ffn_body(x, mod, g_ref, win_ref, wout_ref, fg_ref, acc_ref, final):
    h = _normmod(x, g_ref[...], mod[4:5], mod[3:4]).astype(BF)
    for j in range(DFF // FFN_CH):
        c0 = j * FFN_CH
        u1 = _dot(h, win_ref[:, c0:c0 + FFN_CH])
        u2 = _dot(h, win_ref[:, DFF + c0:DFF + c0 + FFN_CH])
        a = (_silu(u1) * u2).astype(BF)
        part = _dot(a, wout_ref[c0:c0 + FFN_CH, :])
        if j == 0:
            acc_ref[...] = part
        else:
            acc_ref[...] += part
    out = x + mod[5:6] * acc_ref[...]
    if final:
        ms = jnp.mean(out * out, axis=-1, keepdims=True)
        out = out * lax.rsqrt(ms + EPS) * fg_ref[...]
    return out


def _ffn_kernel(x_ref, mod_ref, *refs, final, n_y):
    y_refs = refs[:n_y]
    if n_y:
        wp_ref, bp_ref = refs[n_y:n_y + 2]
        refs = refs[n_y + 2:]
    g_ref, win_ref, wout_ref, fg_ref, o_ref, acc_ref = refs[:6]
    mod = mod_ref[0]
    if n_y == 0:
        x = x_ref[0]
    else:
        x1_ref = refs[6]
        if n_y == 3:
            y = ((y_refs[0][0] + y_refs[1][0]) * y_refs[2][0]).astype(BF)
        else:
            y = y_refs[0][0]
        x1_ref[...] = x_ref[0] + mod[2:3] * (_dot(y, wp_ref[...]) + bp_ref[...])
        x = x1_ref[...]
    o_ref[0] = _ffn_body(x, mod, g_ref, win_ref, wout_ref, fg_ref, acc_ref, final)


def _ffn(x, mod, g, w_in, w_out, layer, final_g=None, ys=(), w_p=None, b_p=None):
    bsz, t, _ = x.shape
    tm = min(1024, t) if not ys else _row_tile(t)
    final = final_g is not None
    fg = (final_g if final else jnp.ones((D,), F32)).reshape(1, D)
    row = pl.BlockSpec((1, tm, D), lambda bi, ti: (bi, ti, 0))
    layer_spec = lambda r, c: pl.BlockSpec((None, r, c), lambda *_: (layer, 0, 0),
                                           pipeline_mode=pl.Buffered(1))
    n_y = len(ys)
    proj_specs = [row] * n_y + ([_const_spec((D, D)), _const_spec((1, D))] if n_y else [])
    proj_args = list(ys) + ([w_p, b_p.reshape(1, D)] if n_y else [])
    scratch = [pltpu.VMEM((tm, D), F32)] * (2 if n_y else 1)
    return pl.pallas_call(
        functools.partial(_ffn_kernel, final=final, n_y=n_y),
        grid=(bsz, t // tm),
        in_specs=[row, pl.BlockSpec((1, SUBLANES, D), lambda bi, ti: (bi, 0, 0))] + proj_specs + [
            _const_spec((1, D)),
            layer_spec(D, 2 * DFF),
            layer_spec(DFF, D),
            _const_spec((1, D)),
        ],
        out_specs=row,
        out_shape=jax.ShapeDtypeStruct((bsz, t, D), F32),
        scratch_shapes=scratch,
        compiler_params=_params(2),
    )(x, mod, *proj_args, g.reshape(1, D), w_in, w_out, fg)


LRU_HALO = 16


def _lru_in_kernel(xp_ref, x_ref, xn_ref, mod_ref, g_ref, w_ref, b_ref, wc_ref, bc_ref,
                   gate_ref, xr_ref, h_ref, u_ref, *, tm, nt):
    t = pl.program_id(1)
    mod = mod_ref[0]
    g, sc, sh = g_ref[...], mod[1:2], mod[0:1]
    h_ref[0:LRU_HALO, :] = _normmod(xp_ref[0], g, sc, sh).astype(BF)
    h_ref[LRU_HALO:LRU_HALO + tm, :] = _normmod(x_ref[0], g, sc, sh).astype(BF)
    h_ref[LRU_HALO + tm:2 * LRU_HALO + tm, :] = _normmod(xn_ref[0], g, sc, sh).astype(BF)
    for j in range(D // PROJ_CH):
        c0 = j * PROJ_CH
        cols = slice(c0, c0 + PROJ_CH)
        gate_ref[0, :, cols] = _gelu_tanh(
            _dot(h_ref[LRU_HALO:LRU_HALO + tm, :], w_ref[:, cols]) + b_ref[:, cols])
        u = _dot(h_ref[...], w_ref[:, D + c0:D + c0 + PROJ_CH]) + b_ref[:, D + c0:D + c0 + PROJ_CH]
        u_ref[0:LRU_HALO, cols] = jnp.where(t > 0, u[0:LRU_HALO], 0.0)
        u_ref[LRU_HALO:LRU_HALO + tm, cols] = u[LRU_HALO:LRU_HALO + tm]
        u_ref[LRU_HALO + tm:, cols] = jnp.where(t < nt - 1, u[LRU_HALO + tm:], 0.0)
    off = LRU_HALO - RNN_CONV_K // 2
    for n in range(D // LANES):
        lanes = slice(n * LANES, (n + 1) * LANES)
        xr = jnp.zeros((tm, LANES), F32)
        for k in range(RNN_CONV_K):
            xr = xr + u_ref[off + k:off + k + tm, lanes] * wc_ref[k:k + 1, lanes]
        xr_ref[0, :, lanes] = xr + bc_ref[:, lanes]


def _lru_in(x, mod, g, w, b, w_conv, b_conv):
    bsz, t, _ = x.shape
    tm = _row_tile(t, 1024)
    nt = t // tm
    hb = tm // LRU_HALO
    row = pl.BlockSpec((1, tm, D), lambda bi, ti: (bi, ti, 0))
    return pl.pallas_call(
        functools.partial(_lru_in_kernel, tm=tm, nt=nt),
        grid=(bsz, nt),
        in_specs=[
            pl.BlockSpec((1, LRU_HALO, D), lambda bi, ti: (bi, jnp.maximum(ti * hb - 1, 0), 0)),
            row,
            pl.BlockSpec((1, LRU_HALO, D),
                         lambda bi, ti: (bi, jnp.minimum((ti + 1) * hb, t // LRU_HALO - 1), 0)),
            pl.BlockSpec((1, SUBLANES, D), lambda bi, ti: (bi, 0, 0)),
            _const_spec((1, D)),
            _const_spec((D, 2 * D)),
            _const_spec((1, 2 * D)),
            _const_spec((RNN_CONV_K, D)),
            _const_spec((1, D)),
        ],
        out_specs=[row, row],
        out_shape=[jax.ShapeDtypeStruct((bsz, t, D), F32)] * 2,
        scratch_shapes=[pltpu.VMEM((tm + 2 * LRU_HALO, D), BF),
                        pltpu.VMEM((tm + 2 * LRU_HALO, D), F32)],
        compiler_params=_params(2),
    )(x, x, x, mod, g.reshape(1, D), w, b.reshape(1, 2 * D), w_conv, b_conv.reshape(1, D))


def _lru_scan_kernel(xr_ref, h0_ref, wg_ref, bg_ref, lam_ref, hs_ref, hl_ref,
                     pc_ref, bs_ref, h_ref, *, tm, reverse):
    t = pl.program_id(1)

    @pl.when(t == 0)
    def _():
        h_ref[...] = jnp.broadcast_to(h0_ref[0], (SUBLANES, D))

    nv = tm // SUBLANES
    sub = lax.broadcasted_iota(jnp.int32, (nv, SUBLANES, LANES), 1)

    for n in range(RNN_BLOCKS):
        lanes = slice(n * LANES, (n + 1) * LANES)
        xr = xr_ref[0, :, lanes]
        gh = _dot(xr.astype(BF), wg_ref[n]) + bg_ref[n]
        tr = jnp.tanh(gh[:, :LANES])
        tg = jnp.tanh(gh[:, LANES:])
        nl = -lam_ref[:, lanes]
        sp = jnp.maximum(nl, 0.0) + jnp.log1p(jnp.exp(-jnp.abs(nl)))
        c1 = (-0.5 * LRU_C) * sp
        log_a = c1 * tr + c1
        a = jnp.exp(log_a)
        th = jnp.tanh(log_a)
        one_m_a2 = (-2.0 * th) / (1.0 - th)
        root = jnp.where(one_m_a2 > 0.0, one_m_a2 * lax.rsqrt(one_m_a2), 0.0)
        hx = 0.5 * xr
        b = root * (hx * tg + hx)
        a3 = a.reshape(nv, SUBLANES, LANES)
        b3 = b.reshape(nv, SUBLANES, LANES)
        for sh in (1, 2, 4):
            if reverse:
                m = sub < SUBLANES - sh
                a_s = pltpu.roll(a3, SUBLANES - sh, 1)
                b_s = pltpu.roll(b3, SUBLANES - sh, 1)
            else:
                m = sub >= sh
                a_s = pltpu.roll(a3, sh, 1)
                b_s = pltpu.roll(b3, sh, 1)
            b3 = jnp.where(m, a3 * b_s + b3, b3)
            a3 = jnp.where(m, a3 * a_s, a3)
        pc_ref[:, lanes] = a3.reshape(tm, LANES)
        bs_ref[:, lanes] = b3.reshape(tm, LANES)

    edge = 0 if reverse else SUBLANES - 1

    def group(i, hcar):
        v = (nv - 1 - i) if reverse else i
        st = pl.multiple_of(v * SUBLANES, SUBLANES)
        res = bs_ref[pl.ds(st, SUBLANES), :] + pc_ref[pl.ds(st, SUBLANES), :] * hcar
        hs_ref[0, pl.ds(st, SUBLANES), :] = res
        return jnp.broadcast_to(res[edge:edge + 1, :], (SUBLANES, D))

    hfin = lax.fori_loop(0, nv, group, h_ref[...])
    h_ref[...] = hfin
    hl_ref[0] = hfin[0:1, :]


def _lru_scan(xr, h0, wg, bg, lam, reverse):
    bsz, t, _ = xr.shape
    tm = _row_tile(t, 1024)
    nt = t // tm
    tidx = (lambda ti: nt - 1 - ti) if reverse else (lambda ti: ti)
    row = pl.BlockSpec((1, tm, D), lambda bi, ti: (bi, tidx(ti), 0))
    hs, hl = pl.pallas_call(
        functools.partial(_lru_scan_kernel, tm=tm, reverse=reverse),
        grid=(bsz, nt),
        in_specs=[
            row,
            pl.BlockSpec((1, 1, D), lambda bi, ti: (bi, 0, 0)),
            _const_spec((RNN_BLOCKS, RNN_BLOCK, 2 * RNN_BLOCK)),
            _const_spec((RNN_BLOCKS, 1, 2 * RNN_BLOCK)),
            _const_spec((1, D)),
        ],
        out_specs=[row, pl.BlockSpec((1, 1, D), lambda bi, ti: (bi, 0, 0))],
        out_shape=[jax.ShapeDtypeStruct((bsz, t, D), F32), jax.ShapeDtypeStruct((bsz, 1, D), F32)],
        scratch_shapes=[pltpu.VMEM((tm, D), F32), pltpu.VMEM((tm, D), F32),
                        pltpu.VMEM((SUBLANES, D), F32)],
        compiler_params=_params(2),
    )(xr, h0, wg, bg, lam.reshape(1, D))
    return hs, hl


NA_QR = 4
NA_SR = 12
NA_HG = 4
NA_NQ = NA_QR * GRID_W
NA_QC = NA_NQ
NA_NK = NA_SR * GRID_W
NA_ROWS = 8192 // GRID_W
NA_PAD = NA_KH // 2
NA_CASES = (0, NA_QR, NA_ROWS - NA_QR)


def _na_case_geometry(r):
    rs = min(max(r - NA_KH // 2, 0), NA_ROWS - NA_SR)
    qr = r + np.arange(NA_QR)
    r0 = np.clip(qr - NA_KH // 2, 0, NA_ROWS - NA_KH)
    kr = rs + np.arange(NA_SR)
    row_ok = (kr[None, :] >= r0[:, None]) & (kr[None, :] < r0[:, None] + NA_KH)
    a0 = [NA_PAD + rs - int(q) + NA_KH - 1 for q in qr]
    return row_ok, a0


def _na_table_kernel(cm2_ref, o_ref):
    lo = lax.broadcasted_iota(jnp.int32, (GRID_W, LANES), 1) < GRID_W
    for ci, r in enumerate(NA_CASES):
        row_ok, a0 = _na_case_geometry(r)
        for qi in range(NA_QR):
            for j in range(NA_SR // 2):
                blk = cm2_ref[0, a0[qi] + 2 * j]
                ok_l, ok_r = bool(row_ok[qi, 2 * j]), bool(row_ok[qi, 2 * j + 1])
                if not (ok_l or ok_r):
                    blk = jnp.full((GRID_W, LANES), NEG, F32)
                elif not ok_r:
                    blk = jnp.where(lo, blk, NEG)
                elif not ok_l:
                    blk = jnp.where(lo, NEG, blk)
                o_ref[ci, 0, qi * GRID_W:(qi + 1) * GRID_W, j * LANES:(j + 1) * LANES] = blk


def _na_bias_tables(rpb):
    cols = np.arange(GRID_W)
    c0 = np.clip(cols - NA_KW // 2, 0, GRID_W - NA_KW)
    col_ok = (cols[None, :] >= c0[:, None]) & (cols[None, :] < c0[:, None] + NA_KW)
    col_off = cols[None, :] - cols[:, None] + (NA_KW - 1)
    onehot = ((col_off[..., None] == np.arange(2 * NA_KW - 1)) & col_ok[..., None]).astype(np.float32)
    cm = jnp.einsum("hab,ckb->hack", rpb, jnp.asarray(onehot), precision=lax.Precision.HIGHEST)
    cm = jnp.where(jnp.asarray(col_ok), cm, NEG)
    cm = jnp.pad(cm, ((0, 0), (NA_PAD, NA_PAD), (0, 0), (0, 0)), constant_values=NEG)
    cm2 = jnp.concatenate([cm[:, :-1], cm[:, 1:]], axis=-1)
    na = cm2.shape[1]
    return pl.pallas_call(
        _na_table_kernel,
        grid=(HEADS,),
        in_specs=[pl.BlockSpec((1, na, GRID_W, LANES), lambda h: (h, 0, 0, 0))],
        out_specs=pl.BlockSpec((len(NA_CASES), 1, NA_NQ, NA_NK), lambda h: (0, h, 0, 0)),
        out_shape=jax.ShapeDtypeStruct((len(NA_CASES), HEADS, NA_NQ, NA_NK), F32),
        compiler_params=_params(1),
    )(cm2)


def _na_kernel(q_ref, k_ref, v_ref, kc_ref, vc_ref, bias_ref, o_ref):
    ri = pl.program_id(2)
    rs = jnp.clip(ri * NA_QR - NA_KH // 2, 0, NA_ROWS - NA_SR)
    st = pl.multiple_of(rs * GRID_W, GRID_W)
    lo = lax.broadcasted_iota(jnp.int32, (1, LANES), 1) < HD
    for hp in range(NA_HG // 2):
        lanes = slice(hp * LANES, (hp + 1) * LANES)
        kp = k_ref[0, pl.ds(st, NA_NK), lanes]
        kcp = kc_ref[0, :, lanes]
        vp = jnp.concatenate([v_ref[0, pl.ds(st, NA_NK), lanes], vc_ref[0, :, lanes]], axis=0)
        for qc in range(NA_NQ // NA_QC):
            rows = slice(qc * NA_QC, (qc + 1) * NA_QC)
            qp = q_ref[0, rows, lanes]
            res = []
            for hh in range(2):
                own = lo if hh == 0 else jnp.logical_not(lo)
                qh = jnp.where(own, qp, jnp.zeros_like(qp))
                s = jnp.concatenate([_dot_nt(qh, kp) + bias_ref[0, 2 * hp + hh, rows, :],
                                     _dot_nt(qh, kcp)], axis=-1)
                p = jnp.exp(s - jnp.max(s, axis=-1, keepdims=True)).astype(BF)
                res.append(_dot(p, jnp.where(own, vp, jnp.ones_like(vp))))
            num = jnp.where(lo, res[0], res[1])
            den = pltpu.roll(jnp.where(lo, res[1], res[0]), HD, 1)
            o_ref[0, rows, lanes] = (num / den).astype(BF)


def _na_attention(qkv, qkv_c, bias_tabs):
    bsz, s, _ = qkv.shape
    nr = NA_ROWS // NA_QR
    hw = NA_HG * HD
    nhg = HEADS // NA_HG

    def case(ri):
        return jnp.where(ri == 0, 0, jnp.where(ri == nr - 1, 2, 1))

    return pl.pallas_call(
        _na_kernel,
        grid=(bsz, nhg, nr),
        in_specs=[
            pl.BlockSpec((1, NA_NQ, hw), lambda bi, hg, ri: (bi, ri, hg)),
            pl.BlockSpec((1, s, hw), lambda bi, hg, ri: (bi, 0, nhg + hg)),
            pl.BlockSpec((1, s, hw), lambda bi, hg, ri: (bi, 0, 2 * nhg + hg)),
            pl.BlockSpec((1, CTX, hw), lambda bi, hg, ri: (bi, 0, nhg + hg)),
            pl.BlockSpec((1, CTX, hw), lambda bi, hg, ri: (bi, 0, 2 * nhg + hg)),
            pl.BlockSpec((1, NA_HG, NA_NQ, NA_NK), lambda bi, hg, ri: (case(ri), hg, 0, 0)),
        ],
        out_specs=pl.BlockSpec((1, NA_NQ, hw), lambda bi, hg, ri: (bi, ri, hg)),
        out_shape=jax.ShapeDtypeStruct((bsz, s, D), BF),
        compiler_params=_params(3),
    )(qkv, qkv, qkv, qkv_c, qkv_c, bias_tabs)


def kernel(x, c, ctx, c_ctx, w_mod, b_mod, norm_g, w_ffn_in, w_ffn_out, a_w_pw1, a_b_pw1, a_w_dw, a_b_dw, a_ln_g, a_ln_b, a_w_pw2, a_b_pw2, b_w_in, b_b_in, b_w_conv, b_b_conv, b_w_rg, b_b_rg, b_w_ig, b_b_ig, b_lam, b_w_out, b_b_out, c_w_qkv, c_b_qkv, c_rpb, c_w_o, c_b_o, final_g):
    bsz = x.shape[0]
    cond = jnp.concatenate([c, c_ctx[None], jnp.zeros((SUBLANES - bsz - 1, D), F32)], axis=0)
    mods = _mod_call(cond, w_mod, b_mod)

    def layer_mods(i):
        m = mods[i].reshape(SUBLANES, 6, D)
        pad = ((0, 0), (0, SUBLANES - 6), (0, 0))
        lat = jnp.pad(m[:bsz], pad)
        cx = jnp.pad(jnp.broadcast_to(m[bsz:bsz + 1], (bsz, 6, D)), pad)
        return lat, cx

    def conv_layer(xs, mod, i, j):
        u = _in_proj(xs, mod, norm_g[i, 0], a_w_pw1[j].astype(BF), a_b_pw1[j], "glu")
        return _conv_block(xs, u, mod, a_w_dw[j], a_b_dw[j], a_ln_g[j], a_ln_b[j],
                           a_w_pw2[j].astype(BF), a_b_pw2[j])

    w_ffn_in_bf = w_ffn_in.astype(BF)
    w_ffn_out_bf = w_ffn_out.astype(BF)

    def ffn(xs, mod, i, final_g=None, **proj):
        return _ffn(xs, mod, norm_g[i, 1], w_ffn_in_bf, w_ffn_out_bf, i, final_g, **proj)

    cs = ctx
    flat = lambda a: a.reshape(1, bsz * CTX, a.shape[-1])
    unflat = lambda a: a.reshape(bsz, CTX, a.shape[-1])

    lat, cx = layer_mods(0)
    x = ffn(conv_layer(x, lat, 0, 0), lat, 0)
    u_c = unflat(_in_proj(flat(cs), cx[:1], norm_g[0, 0], a_w_pw1[0].astype(BF), a_b_pw1[0], "glu"))
    cs = _conv_block(cs, u_c, cx, a_w_dw[0], a_b_dw[0], a_ln_g[0], a_ln_b[0],
                     a_w_pw2[0].astype(BF), a_b_pw2[0])
    cs = unflat(ffn(flat(cs), cx[:1], 0))

    lat, cx = layer_mods(1)
    w_in = b_w_in[0].astype(BF)
    w_out = b_w_out[0].astype(BF)
    gate_l, xr_l = _lru_in(x, lat, norm_g[1, 0], w_in, b_b_in[0], b_w_conv[0], b_b_conv[0])
    gate_c, xr_c = _lru_in(cs, cx, norm_g[1, 0], w_in, b_b_in[0], b_w_conv[0], b_b_conv[0])
    hs_l, hs_c = [], []
    for d, reverse in ((0, False), (1, True)):
        wg = (0.5 * jnp.concatenate([b_w_rg[0, d], b_w_ig[0, d]], axis=-1)).astype(BF)
        bg = 0.5 * jnp.concatenate([b_b_rg[0, d].reshape(RNN_BLOCKS, 1, RNN_BLOCK),
                                    b_b_ig[0, d].reshape(RNN_BLOCKS, 1, RNN_BLOCK)], axis=-1)
        h0 = jnp.zeros((bsz, 1, D), F32)
        hc, hc_last = _lru_scan(xr_c, h0, wg, bg, b_lam[0, d], reverse)
        hl, _ = _lru_scan(xr_l, hc_last, wg, bg, b_lam[0, d], reverse)
        hs_c.append(hc)
        hs_l.append(hl)
    x = ffn(x, lat, 1, ys=(hs_l[0], hs_l[1], gate_l), w_p=w_out, b_p=b_b_out[0])
    cs = unflat(ffn(flat(cs), cx[:1], 1, ys=(flat(hs_c[0]), flat(hs_c[1]), flat(gate_c)),
                    w_p=w_out, b_p=b_b_out[0]))

    lat, cx = layer_mods(2)
    w_qkv = c_w_qkv[0].astype(BF)
    qkv = _in_proj(x, lat, norm_g[2, 0], w_qkv, c_b_qkv[0], "qkv")
    qkv_c = unflat(_in_proj(flat(cs), cx[:1], norm_g[2, 0], w_qkv, c_b_qkv[0], "qkv"))
    o = _na_attention(qkv, qkv_c, _na_bias_tables(c_rpb[0]))
    x = ffn(x, lat, 2, ys=(o,), w_p=c_w_o[0].astype(BF), b_p=c_b_o[0])

    lat, _ = layer_mods(3)
    return ffn(conv_layer(x, lat, 3, 1), lat, 3, final_g)
```
